```python
import functools
import jax, jax.numpy as jnp
from jax import lax
import numpy as np

D_MODEL = 4096
BATCH = 2
SEQ = 4096
DEPTH = 1
DEC_BATCH = 128
DEC_SEQ = 1
PAST_LEN = 2048
PAGE_SIZE = 128

N_META = 16
N_HEADS = 16
HEAD_DIM = 128
ATTN_WIDTH = N_HEADS * HEAD_DIM
HEAD_OUT = D_MODEL // N_HEADS
ATTN_SCALE = HEAD_DIM ** -0.5
Q_BLOCK = 128
POOL_WINDOWS = (2, 4, 8, 16)
N_POOL_GROUPS = len(POOL_WINDOWS)
POOL_WIDTH = D_MODEL // 2
POOL_GROUP = POOL_WIDTH // N_POOL_GROUPS
POOL_GROUP_OUT = D_MODEL // N_POOL_GROUPS
POOL_STATE = max(POOL_WINDOWS) - 1
D_FF = 256 * ((8 * D_MODEL // 3 + 255) // 256)
CONV_W = 3
RMS_EPS = 1e-6
IN_SPLITS = (ATTN_WIDTH, ATTN_WIDTH, ATTN_WIDTH, N_HEADS, POOL_WIDTH, D_MODEL, D_MODEL)
IN_COLS = sum(IN_SPLITS)

kernel_name = 'fox_pool_hybrid_step'


def _rmsnorm(x, g):
    xf = x.astype(jnp.float32)
    y = xf * lax.rsqrt(jnp.mean(xf * xf, axis=-1, keepdims=True) + RMS_EPS)
    return (y * g.astype(jnp.float32)).astype(x.dtype)


def _project_in(h, w_in, b_forget):
    B, T, _ = h.shape
    z = h @ w_in
    offs = [0]
    for s in IN_SPLITS:
        offs.append(offs[-1] + s)
    q, k, v, fl, u, ga, gp = [z[..., offs[i]:offs[i + 1]] for i in range(len(IN_SPLITS))]
    q = q.reshape(B, T, N_HEADS, HEAD_DIM)
    k = k.reshape(B, T, N_HEADS, HEAD_DIM)
    v = v.reshape(B, T, N_HEADS, HEAD_DIM)
    logf = jax.nn.log_sigmoid((fl + b_forget).astype(jnp.float32))
    return q, k, v, logf, u, ga, gp


def _attend(q, k, v, cq, ck, q_pos, k_pos):
    s = jnp.einsum('bqhd,bkhd->bhqk', q, k).astype(jnp.float32) * ATTN_SCALE
    s = s + jnp.swapaxes(cq, 1, 2)[..., :, None] - jnp.swapaxes(ck, 1, 2)[..., None, :]
    s = jnp.where(k_pos[None, :] <= q_pos[:, None], s, -jnp.inf)
    p = jax.nn.softmax(s, axis=-1).astype(v.dtype)
    return jnp.einsum('bhqk,bkhd->bqhd', p, v)


def _forget_attn_prompt(q, k, v, logf):
    B, L = q.shape[:2]
    c = jnp.cumsum(logf, axis=1)
    pos = jnp.arange(L)
    o_meta = _attend(q[:, :N_META], k[:, :N_META], v[:, :N_META], c[:, :N_META], c[:, :N_META],
                     pos[:N_META], pos[:N_META])
    nb = (L - N_META) // Q_BLOCK
    qb = q[:, N_META:].reshape(B, nb, Q_BLOCK, N_HEADS, HEAD_DIM).transpose(1, 0, 2, 3, 4)
    cb = c[:, N_META:].reshape(B, nb, Q_BLOCK, N_HEADS).transpose(1, 0, 2, 3)
    pb = pos[N_META:].reshape(nb, Q_BLOCK)

    def one_block(args):
        q_blk, c_blk, p_blk = args
        return _attend(q_blk, k, v, c_blk, c, p_blk, pos)

    o = lax.map(one_block, (qb, cb, pb))
    o = o.transpose(1, 0, 2, 3, 4).reshape(B, L - N_META, N_HEADS, HEAD_DIM)
    return jnp.concatenate([o_meta, o], axis=1)


def _forget_attn_sample(q, k, v, logf, cache_k, cache_v, cache_logf, page_table):
    DB, T = q.shape[:2]
    P = page_table.shape[1] * PAGE_SIZE
    kp = cache_k[page_table].reshape(DB, P, N_HEADS, HEAD_DIM).astype(q.dtype)
    vp = cache_v[page_table].reshape(DB, P, N_HEADS, HEAD_DIM).astype(v.dtype)
    lfp = cache_logf[page_table].reshape(DB, P, N_HEADS).astype(jnp.float32)
    c = jnp.swapaxes(jnp.cumsum(jnp.concatenate([lfp, logf], axis=1), axis=1), 1, 2)
    cq = c[..., P:, None]
    s_past = jnp.einsum('bqhd,bkhd->bhqk', q, kp).astype(jnp.float32) * ATTN_SCALE + cq - c[..., None, :P]
    s_new = jnp.einsum('bqhd,bkhd->bhqk', q, k).astype(jnp.float32) * ATTN_SCALE + cq - c[..., None, P:]
    i = jnp.arange(T)
    s_new = jnp.where(i[None, :] <= i[:, None], s_new, -jnp.inf)
    p = jax.nn.softmax(jnp.concatenate([s_past, s_new], axis=-1), axis=-1)
    return (jnp.einsum('bhqk,bkhd->bqhd', p[..., :P].astype(vp.dtype), vp)
            + jnp.einsum('bhqk,bkhd->bqhd', p[..., P:].astype(v.dtype), v))


def _pool_mix(u_ext, n_prev, start_pos, w_pool_map, pool_scale):
    B, n_tot, _ = u_ext.shape
    T = n_tot - n_prev
    cs = jnp.concatenate([jnp.zeros((B, 1, POOL_WIDTH), jnp.float32),
                          jnp.cumsum(u_ext.astype(jnp.float32), axis=1)], axis=1)
    i = jnp.arange(T)
    hi = n_prev + i + 1
    pos = start_pos + i
    means = []
    for g, w in enumerate(POOL_WINDOWS):
        cnt = jnp.minimum(pos + 1, w)
        cs_g = cs[..., g * POOL_GROUP:(g + 1) * POOL_GROUP]
        means.append((cs_g[:, hi] - cs_g[:, hi - cnt]) / cnt[None, :, None].astype(jnp.float32))
    d = jnp.concatenate(means, axis=-1) - u_ext[:, n_prev:].astype(jnp.float32)
    d = d.astype(u_ext.dtype).reshape(B, T, N_POOL_GROUPS, POOL_GROUP)
    out = jnp.einsum('btgc,gce->btge', d, w_pool_map).reshape(B, T, D_MODEL)
    return out * pool_scale


def _conv_ffn(h, buf, w_gate, w_up, conv_w, conv_b, w_down):
    T = h.shape[1]
    a = h @ w_gate
    b = h @ w_up
    a_ext = jnp.concatenate([buf.astype(a.dtype), a], axis=1)
    conv = conv_b + conv_w[0] * a_ext[:, 0:T]
    for j in range(1, CONV_W):
        conv = conv + conv_w[j] * a_ext[:, j:j + T]
    y = (jax.nn.gelu(conv) * b) @ w_down
    return y, a_ext[:, -(CONV_W - 1):]


def _block(x, attend, pool_buf, conv_buf, start_pos, g_pre_mix, g_post_mix, g_pre_ffn, g_post_ffn,
           w_in, b_forget, w_head_out, w_pool_map, pool_scale, w_out, w_gate, w_up, conv_w, conv_b, w_down):
    B, T, _ = x.shape
    h = _rmsnorm(x, g_pre_mix)
    q, k, v, logf, u, ga, gp = _project_in(h, w_in, b_forget)
    o = attend(q, k, v, logf)
    a = jnp.einsum('bthd,hde->bthe', o, w_head_out).reshape(B, T, D_MODEL)
    u_ext = jnp.concatenate([pool_buf.astype(u.dtype), u], axis=1)
    p = _pool_mix(u_ext, pool_buf.shape[1], start_pos, w_pool_map, pool_scale)
    m = (jax.nn.sigmoid(ga) * a + jax.nn.sigmoid(gp) * p) @ w_out
    x = x + _rmsnorm(m, g_post_mix)
    f, conv_state = _conv_ffn(_rmsnorm(x, g_pre_ffn), conv_buf, w_gate, w_up, conv_w, conv_b, w_down)
    x = x + _rmsnorm(f, g_post_ffn)
    return x, (k, v, logf, u_ext[:, -POOL_STATE:], conv_state)


def setup_inputs(seed: int = 0) -> dict:
    key = jax.random.key(seed)
    ks = jax.random.split(key, 24)
    f32 = jnp.float32
    n_pages = PAST_LEN // PAGE_SIZE
    n_used = DEC_BATCH * n_pages
    n_phys = n_used + max(1, n_used // 4)

    def nrm(k, shape, scale=1.0):
        return jax.random.normal(k, shape, f32) * scale

    def gain(k):
        return 1.0 + 0.05 * jax.random.normal(k, (DEPTH, D_MODEL), f32)

    page_table = jax.random.permutation(ks[7], n_phys)[:n_used].reshape(DEC_BATCH, n_pages).astype(jnp.int32)
    return {
        'x_prompt': nrm(ks[0], (BATCH, SEQ, D_MODEL)),
        'x_sample': nrm(ks[1], (DEC_BATCH, DEC_SEQ, D_MODEL)),
        'cache_k': nrm(ks[2], (DEPTH, n_phys, PAGE_SIZE, N_HEADS, HEAD_DIM)),
        'cache_v': nrm(ks[3], (DEPTH, n_phys, PAGE_SIZE, N_HEADS, HEAD_DIM)),
        'cache_logf': jax.nn.log_sigmoid(nrm(ks[4], (DEPTH, n_phys, PAGE_SIZE, N_HEADS)) + 3.0),
        'state_pool': nrm(ks[5], (DEPTH, DEC_BATCH, POOL_STATE, POOL_WIDTH)),
        'state_conv': nrm(ks[6], (DEPTH, DEC_BATCH, CONV_W - 1, D_FF)),
        'page_table': page_table,
        'meta_tokens': nrm(ks[8], (N_META, D_MODEL)),
        'g_pre_mix': gain(ks[9]),
        'g_post_mix': gain(ks[10]),
        'g_pre_ffn': gain(ks[11]),
        'g_post_ffn': gain(ks[12]),
        'w_in': nrm(ks[13], (DEPTH, D_MODEL, IN_COLS), D_MODEL ** -0.5),
        'b_forget': jax.random.uniform(ks[14], (DEPTH, N_HEADS), f32, 1.0, 5.0),
        'w_head_out': nrm(ks[15], (DEPTH, N_HEADS, HEAD_DIM, HEAD_OUT), HEAD_DIM ** -0.5),
        'w_pool_map': nrm(ks[16], (DEPTH, N_POOL_GROUPS, POOL_GROUP, POOL_GROUP_OUT), POOL_GROUP ** -0.5),
        'pool_scale': gain(ks[17]),
        'w_out': nrm(ks[18], (DEPTH, D_MODEL, D_MODEL), D_MODEL ** -0.5),
        'w_gate': nrm(ks[19], (DEPTH, D_MODEL, D_FF), D_MODEL ** -0.5),
        'w_up': nrm(ks[20], (DEPTH, D_MODEL, D_FF), D_MODEL ** -0.5),
        'conv_w': nrm(ks[21], (DEPTH, CONV_W, D_FF), CONV_W ** -0.5),
        'conv_b': nrm(ks[22], (DEPTH, D_FF), 0.02),
        'w_down': nrm(ks[23], (DEPTH, D_FF, D_MODEL), D_FF ** -0.5),
    }


def reference(x_prompt, x_sample, cache_k, cache_v, cache_logf, state_pool, state_conv, page_table,
              meta_tokens, g_pre_mix, g_post_mix, g_pre_ffn, g_post_ffn, w_in, b_forget, w_head_out,
              w_pool_map, pool_scale, w_out, w_gate, w_up, conv_w, conv_b, w_down):
    B = x_prompt.shape[0]
    meta = jnp.broadcast_to(meta_tokens[None].astype(x_prompt.dtype), (B, N_META, D_MODEL))
    xp = jnp.concatenate([meta, x_prompt], axis=1)
    xs = x_sample
    kp_l, vp_l, lfp_l, poolp_l, convp_l = [], [], [], [], []
    ks_l, vs_l, lfs_l, pools_l, convs_l = [], [], [], [], []
    for l in range(DEPTH):
        lw = (g_pre_mix[l], g_post_mix[l], g_pre_ffn[l], g_post_ffn[l], w_in[l], b_forget[l],
              w_head_out[l], w_pool_map[l], pool_scale[l], w_out[l], w_gate[l], w_up[l],
              conv_w[l], conv_b[l], w_down[l])
        xp, (kp, vp, lfp, poolp, convp) = _block(
            xp, _forget_attn_prompt,
            jnp.zeros((B, 0, POOL_WIDTH), xp.dtype), jnp.zeros((B, CONV_W - 1, D_FF), xp.dtype),
            0, *lw)
        attend_s = functools.partial(_forget_attn_sample, cache_k=cache_k[l], cache_v=cache_v[l],
                                     cache_logf=cache_logf[l], page_table=page_table)
        xs, (ksn, vsn, lfs, pools, convs) = _block(xs, attend_s, state_pool[l], state_conv[l], PAST_LEN, *lw)
        kp_l.append(kp); vp_l.append(vp); lfp_l.append(lfp); poolp_l.append(poolp); convp_l.append(convp)
        ks_l.append(ksn); vs_l.append(vsn); lfs_l.append(lfs); pools_l.append(pools); convs_l.append(convs)
    y_prompt = xp[:, N_META:]
    k_prompt = jnp.stack(kp_l); v_prompt = jnp.stack(vp_l); logf_prompt = jnp.stack(lfp_l)
    pool_prompt = jnp.stack(poolp_l); conv_prompt = jnp.stack(convp_l)
    k_sample = jnp.stack(ks_l); v_sample = jnp.stack(vs_l); logf_sample = jnp.stack(lfs_l)
    pool_sample = jnp.stack(pools_l); conv_sample = jnp.stack(convs_l)
    return (y_prompt, xs, k_prompt, v_prompt, logf_prompt, pool_prompt, conv_prompt,
            k_sample, v_sample, logf_sample, pool_sample, conv_sample)
```

```python
import functools
import math

import jax
import jax.numpy as jnp
from jax import lax
from jax.experimental import pallas as pl
from jax.experimental.pallas import tpu as pltpu

F32 = jnp.float32
BF16 = jnp.bfloat16
RMS_EPS = 1e-6
POOL_WINDOWS = (2, 4, 8, 16)
LANES = 128
BF16_SUBLANES = 16
HALO = 16
CONV_TAIL = 8
MASKED_CUMLOG = 1e30
VMEM_LIMIT = 56 * 1024 * 1024


def _pick_tile(n, cap, mult):
    best = None
    for t in range(mult, min(n, cap) + 1, mult):
        if n % t == 0:
            best = t
    return n if best is None else best


def _params(*sem):
    return pltpu.CompilerParams(dimension_semantics=sem, vmem_limit_bytes=VMEM_LIMIT)


def _rms(x, g):
    return x * lax.rsqrt(jnp.mean(x * x, axis=-1, keepdims=True) + RMS_EPS) * g


def _sigmoid(x):
    return 1.0 / (1.0 + jnp.exp(-x))


def _log_sigmoid(x):
    return jnp.minimum(x, 0.0) - jnp.log(1.0 + jnp.exp(-jnp.abs(x)))


def _gelu_tanh(x):
    cdf = 0.5 * (1.0 + jnp.tanh(math.sqrt(2.0 / math.pi) * (x + 0.044715 * (x * x * x))))
    return x * cdf


def _split3(x):
    hi = x.astype(BF16)
    r = x - hi.astype(F32)
    mid = r.astype(BF16)
    lo = (r - mid.astype(F32)).astype(BF16)
    return hi, mid, lo


_NN = (((1,), (0,)), ((), ()))
_NT = (((1,), (1,)), ((), ()))


def _dot_f32_lhs(x, m_bf, dims):
    out = None
    for piece in _split3(x):
        t = lax.dot_general(piece, m_bf, dims, preferred_element_type=F32)
        out = t if out is None else out + t
    return out


def _dot_f32_rhs(m_bf, x, dims):
    out = None
    for piece in _split3(x):
        t = lax.dot_general(m_bf, piece, dims, preferred_element_type=F32)
        out = t if out is None else out + t
    return out


def _rmsnorm_kernel(x_ref, g_ref, o_ref):
    o_ref[...] = _rms(x_ref[...], g_ref[...]).astype(o_ref.dtype)


def _rmsnorm(x, g, out_dtype):
    m, d = x.shape
    tr = _pick_tile(m, 256, BF16_SUBLANES)
    return pl.pallas_call(
        _rmsnorm_kernel,
        grid=(m // tr,),
        in_specs=[pl.BlockSpec((tr, d), lambda i: (i, 0)), pl.BlockSpec((1, d), lambda i: (0, 0))],
        out_specs=pl.BlockSpec((tr, d), lambda i: (i, 0)),
        out_shape=jax.ShapeDtypeStruct((m, d), out_dtype),
        compiler_params=_params("parallel"),
        name="rmsnorm",
    )(x, g)


def _post_mix_kernel(m_ref, x_ref, g1_ref, g2_ref, x2_ref, h2_ref):
    x2 = x_ref[...] + _rms(m_ref[...], g1_ref[...])
    x2_ref[...] = x2
    h2_ref[...] = _rms(x2, g2_ref[...]).astype(h2_ref.dtype)


def _post_mix(mm, x, g1, g2):
    m, d = x.shape
    tr = _pick_tile(m, 256, BF16_SUBLANES)
    row = pl.BlockSpec((tr, d), lambda i: (i, 0))
    vec = pl.BlockSpec((1, d), lambda i: (0, 0))
    return pl.pallas_call(
        _post_mix_kernel,
        grid=(m // tr,),
        in_specs=[row, row, vec, vec],
        out_specs=[row, row],
        out_shape=[jax.ShapeDtypeStruct((m, d), F32), jax.ShapeDtypeStruct((m, d), BF16)],
        compiler_params=_params("parallel"),
        name="post_mix",
    )(mm, x, g1, g2)


def _post_ffn_kernel(f_ref, x_ref, g_ref, y_ref):
    y_ref[...] = x_ref[...] + _rms(f_ref[...], g_ref[...])


def _post_ffn(f, x, g):
    m, d = x.shape
    tr = _pick_tile(m, 256, 8)
    row = pl.BlockSpec((tr, d), lambda i: (i, 0))
    vec = pl.BlockSpec((1, d), lambda i: (0, 0))
    return pl.pallas_call(
        _post_ffn_kernel,
        grid=(m // tr,),
        in_specs=[row, row, vec],
        out_specs=row,
        out_shape=jax.ShapeDtypeStruct((m, d), F32),
        compiler_params=_params("parallel"),
        name="post_ffn",
    )(f, x, g)


def _mm_kernel(x_ref, w_ref, o_ref):
    o_ref[...] = jnp.dot(x_ref[...], w_ref[...], preferred_element_type=F32).astype(o_ref.dtype)


def _matmul(x, w, out_dtype=F32):
    m, k = x.shape
    n = w.shape[1]
    tm = _pick_tile(m, 1408, BF16_SUBLANES)
    tn = _pick_tile(n, 512, LANES)
    return pl.pallas_call(
        _mm_kernel,
        grid=(m // tm, n // tn),
        in_specs=[pl.BlockSpec((tm, k), lambda i, j: (i, 0)), pl.BlockSpec((k, tn), lambda i, j: (0, j))],
        out_specs=pl.BlockSpec((tm, tn), lambda i, j: (i, j)),
        out_shape=jax.ShapeDtypeStruct((m, n), out_dtype),
        compiler_params=_params("parallel", "arbitrary"),
        name="matmul",
    )(x, w)


def _forget_kernel(h_ref, wf_ref, b_ref, lf_ref, c_ref, carry_ref, *, tc, lp, pad):
    i = pl.program_id(0)
    off = lax.rem(i * tc, lp)
    fl = lax.dot_general(wf_ref[...], h_ref[...], _NT, preferred_element_type=F32)
    lf = _log_sigmoid(fl + b_ref[...])
    pos = off + lax.broadcasted_iota(jnp.int32, (1, tc), 1)
    valid = pos >= pad
    lfm = jnp.where(valid, lf, 0.0)
    upper = (lax.broadcasted_iota(jnp.int32, (tc, tc), 0)
             <= lax.broadcasted_iota(jnp.int32, (tc, tc), 1)).astype(BF16)
    cs = _dot_f32_lhs(lfm, upper, _NN)

    @pl.when(off == 0)
    def _():
        carry_ref[...] = jnp.zeros_like(carry_ref)

    c = cs + carry_ref[:, 0:1]
    carry_ref[...] = jnp.broadcast_to(c[:, tc - 1:tc], carry_ref.shape)
    lf_ref[...] = lf
    c_ref[...] = jnp.where(valid, c, MASKED_CUMLOG)


def _forget_prompt(h, wf_t, b_col, lp, pad):
    mp, d = h.shape
    nh = wf_t.shape[0]
    tc = _pick_tile(lp, 384, LANES)
    kern = functools.partial(_forget_kernel, tc=tc, lp=lp, pad=pad)
    return pl.pallas_call(
        kern,
        grid=(mp // tc,),
        in_specs=[pl.BlockSpec((tc, d), lambda i: (i, 0)),
                  pl.BlockSpec((nh, d), lambda i: (0, 0)),
                  pl.BlockSpec((nh, 1), lambda i: (0, 0))],
        out_specs=[pl.BlockSpec((nh, tc), lambda i: (0, i)), pl.BlockSpec((nh, tc), lambda i: (0, i))],
        out_shape=[jax.ShapeDtypeStruct((nh, mp), F32), jax.ShapeDtypeStruct((nh, mp), F32)],
        scratch_shapes=[pltpu.VMEM((nh, LANES), F32)],
        compiler_params=_params("arbitrary"),
        name="forget_prompt",
    )(h, wf_t, b_col)


def _attn_kernel(q_ref, k_ref, v_ref, c_ref, who_ref, a_ref, kb_ref, vb_ref, *, t, scale):
    iq = pl.program_id(2)

    @pl.when(iq == 0)
    def _():
        kb_ref[...] = k_ref[...].astype(BF16)
        vb_ref[...] = v_ref[...].astype(BF16)

    q = q_ref[...].astype(BF16)
    dh = q.shape[1]
    rows = iq * t + lax.broadcasted_iota(jnp.int32, (t, 1), 0)

    def body(j, carry):
        m, l, acc = carry
        start = pl.multiple_of(j * t, t)
        ks = kb_ref[pl.ds(start, t), :]
        vs = vb_ref[pl.ds(start, t), :]
        s = lax.dot_general(q, ks, _NT, preferred_element_type=F32)
        tt = s * scale - c_ref[0, j]
        cols = j * t + lax.broadcasted_iota(jnp.int32, (1, t), 1)
        tt = jnp.where(cols <= rows, tt, -jnp.inf)
        m_new = jnp.maximum(m, jnp.max(tt, axis=-1, keepdims=True))
        alpha = jnp.exp(m - m_new)
        p = jnp.exp(tt - m_new)
        l = alpha * l + jnp.sum(p, axis=-1, keepdims=True)
        acc = alpha * acc + jnp.dot(p.astype(BF16), vs, preferred_element_type=F32)
        return m_new, l, acc

    init = (jnp.full((t, 1), -jnp.inf, F32), jnp.zeros((t, 1), F32), jnp.zeros((t, dh), F32))
    _, l, acc = lax.fori_loop(0, iq + 1, body, init)
    o = (acc / l).astype(BF16)
    a_ref[...] = jnp.dot(o, who_ref[0], preferred_element_type=F32)


def _attn_prompt(zqkv, c_t, who, nb, lp, nh, dh):
    mp = zqkv.shape[0]
    ho = who.shape[2]
    t = _pick_tile(lp, 384, LANES)
    nq = lp // t
    c4 = c_t.reshape(nh, mp // t, 1, t)
    kern = functools.partial(_attn_kernel, t=t, scale=dh ** -0.5)
    return pl.pallas_call(
        kern,
        grid=(nb, nh, nq),
        in_specs=[pl.BlockSpec((t, dh), lambda b, h, i: (b * nq + i, h)),
                  pl.BlockSpec((lp, dh), lambda b, h, i: (b, nh + h)),
                  pl.BlockSpec((lp, dh), lambda b, h, i: (b, 2 * nh + h)),
                  pl.BlockSpec((1, nq, 1, t), lambda b, h, i: (h, b, 0, 0)),
                  pl.BlockSpec((1, dh, ho), lambda b, h, i: (h, 0, 0))],
        out_specs=pl.BlockSpec((t, ho), lambda b, h, i: (b * nq + i, h)),
        out_shape=jax.ShapeDtypeStruct((mp, nh * ho), F32),
        scratch_shapes=[pltpu.VMEM((lp, dh), BF16), pltpu.VMEM((lp, dh), BF16)],
        compiler_params=_params("parallel", "parallel", "arbitrary"),
        name="attn_prompt",
    )(zqkv, zqkv, zqkv, c4, who)


def _sattn_kernel(pt_ref, qkv_q_ref, qkv_k_ref, qkv_v_ref, fl_ref, bf_ref, kc_ref, vc_ref, lc_ref,
                  o_ref, lf_ref, qblk_ref, m_ref, l_ref, carry_ref, acc_ref, *, nh, dh, ps, scale):
    del pt_ref
    p = pl.program_id(1)
    aw = nh * dh
    d_in_head = (lax.broadcasted_iota(jnp.int32, (nh, aw), 1)
                 - dh * lax.broadcasted_iota(jnp.int32, (nh, aw), 0))
    own = (d_in_head >= 0) & (d_in_head < dh)
    eye = (lax.broadcasted_iota(jnp.int32, (nh, nh), 0)
           == lax.broadcasted_iota(jnp.int32, (nh, nh), 1)).astype(BF16)
    upper = (lax.broadcasted_iota(jnp.int32, (ps, ps), 0)
             <= lax.broadcasted_iota(jnp.int32, (ps, ps), 1)).astype(BF16)

    @pl.when(p == 0)
    def _():
        qrow = qkv_q_ref[0]
        qblk_ref[...] = jnp.where(own, jnp.broadcast_to(qrow, (nh, aw)), 0.0).astype(BF16)
        m_ref[...] = jnp.full_like(m_ref, -jnp.inf)
        l_ref[...] = jnp.zeros_like(l_ref)
        carry_ref[...] = jnp.zeros_like(carry_ref)
        acc_ref[...] = jnp.zeros_like(acc_ref)

    def process(kp, vp, lfp, n_valid):
        s = lax.dot_general(qblk_ref[...], kp, _NT, preferred_element_type=F32)
        lf_t = _dot_f32_rhs(eye, lfp, _NT)
        c = _dot_f32_lhs(lf_t, upper, _NN) + carry_ref[:, 0:1]
        carry_ref[...] = jnp.broadcast_to(c[:, ps - 1:ps], carry_ref.shape)
        tt = s * scale - c
        if n_valid is not None:
            tt = jnp.where(lax.broadcasted_iota(jnp.int32, (nh, ps), 1) < n_valid, tt, -jnp.inf)
        m_old = m_ref[:, 0:1]
        m_new = jnp.maximum(m_old, jnp.max(tt, axis=-1, keepdims=True))
        alpha = jnp.exp(m_old - m_new)
        pr = jnp.exp(tt - m_new)
        l_ref[...] = jnp.broadcast_to(alpha * l_ref[:, 0:1] + jnp.sum(pr, axis=-1, keepdims=True),
                                      l_ref.shape)
        acc_ref[...] = alpha * acc_ref[...] + jnp.dot(pr.astype(BF16), vp, preferred_element_type=F32)
        m_ref[...] = jnp.broadcast_to(m_new, m_ref.shape)

    process(kc_ref[0].astype(BF16), vc_ref[0].astype(BF16), lc_ref[0], None)

    @pl.when(p == pl.num_programs(1) - 1)
    def _():
        lf_new = _log_sigmoid(fl_ref[0][:, 0:nh] + bf_ref[...])
        lf_ref[0] = lf_new
        first = lax.broadcasted_iota(jnp.int32, (ps, 1), 0) == 0
        kp = jnp.where(first, jnp.broadcast_to(qkv_k_ref[0], (ps, aw)), 0.0).astype(BF16)
        vp = jnp.where(first, jnp.broadcast_to(qkv_v_ref[0], (ps, aw)), 0.0).astype(BF16)
        lfp = jnp.where(first, jnp.broadcast_to(lf_new, (ps, nh)), 0.0)
        process(kp, vp, lfp, 1)
        o = jnp.where(own, acc_ref[...] / l_ref[:, 0:1], 0.0)
        o_ref[0] = jnp.sum(o, axis=0, keepdims=True)


def _attn_sample(page_table, zqkv_s, fl_s, bf_row, cache_k, cache_v, cache_lf, nh, dh):
    db, n_pages = page_table.shape
    n_phys, ps = cache_k.shape[0], cache_k.shape[1]
    aw = nh * dh
    qkv3 = zqkv_s.reshape(db, 1, 3 * aw)
    fl3 = fl_s.reshape(db, 1, fl_s.shape[1])
    kern = functools.partial(_sattn_kernel, nh=nh, dh=dh, ps=ps, scale=dh ** -0.5)
    grid_spec = pltpu.PrefetchScalarGridSpec(
        num_scalar_prefetch=1,
        grid=(db, n_pages),
        in_specs=[pl.BlockSpec((1, 1, aw), lambda b, p, pt: (b, 0, 0)),
                  pl.BlockSpec((1, 1, aw), lambda b, p, pt: (b, 0, 1)),
                  pl.BlockSpec((1, 1, aw), lambda b, p, pt: (b, 0, 2)),
                  pl.BlockSpec((1, 1, fl3.shape[2]), lambda b, p, pt: (b, 0, 0)),
                  pl.BlockSpec((1, nh), lambda b, p, pt: (0, 0)),
                  pl.BlockSpec((1, ps, aw), lambda b, p, pt: (pt[b, p], 0, 0)),
                  pl.BlockSpec((1, ps, aw), lambda b, p, pt: (pt[b, p], 0, 0)),
                  pl.BlockSpec((1, ps, nh), lambda b, p, pt: (pt[b, p], 0, 0))],
        out_specs=[pl.BlockSpec((1, 1, aw), lambda b, p, pt: (b, 0, 0)),
                   pl.BlockSpec((1, 1, nh), lambda b, p, pt: (b, 0, 0))],
        scratch_shapes=[pltpu.VMEM((nh, aw), BF16), pltpu.VMEM((nh, LANES), F32),
                        pltpu.VMEM((nh, LANES), F32), pltpu.VMEM((nh, LANES), F32),
                        pltpu.VMEM((nh, aw), F32)],
    )
    o3, lf3 = pl.pallas_call(
        kern,
        grid_spec=grid_spec,
        out_shape=[jax.ShapeDtypeStruct((db, 1, aw), F32), jax.ShapeDtypeStruct((db, 1, nh), F32)],
        compiler_params=_params("parallel", "arbitrary"),
        name="attn_sample",
    )(page_table, qkv3, qkv3, qkv3, fl3, bf_row,
      cache_k.reshape(n_phys, ps, aw), cache_v.reshape(n_phys, ps, aw), cache_lf)
    return o3.reshape(db, aw), lf3.reshape(db, nh)


def _head_out_kernel(o_ref, w_ref, a_ref):
    a_ref[...] = jnp.dot(o_ref[...].astype(BF16), w_ref[0], preferred_element_type=F32)


def _head_out(o, who):
    m = o.shape[0]
    nh, dh, ho = who.shape
    return pl.pallas_call(
        _head_out_kernel,
        grid=(nh,),
        in_specs=[pl.BlockSpec((m, dh), lambda h: (0, h)), pl.BlockSpec((1, dh, ho), lambda h: (h, 0, 0))],
        out_specs=pl.BlockSpec((m, ho), lambda h: (0, h)),
        out_shape=jax.ShapeDtypeStruct((m, nh * ho), F32),
        compiler_params=_params("parallel"),
        name="head_out",
    )(o, who)


def _pool_prompt_kernel(halo_ref, u_ref, d_ref, ext_ref, *, tr, lp, pad, pg):
    i = pl.program_id(0)
    ext_ref[0:HALO, :] = halo_ref[...]
    ext_ref[HALO:, :] = u_ref[...]
    pos = lax.rem(i * tr, lp) - pad + lax.broadcasted_iota(jnp.int32, (tr, 1), 0)
    for g, w in enumerate(POOL_WINDOWS):
        cols = slice(g * pg, (g + 1) * pg)
        u = ext_ref[pl.ds(HALO, tr), cols]
        acc = u
        for k in range(1, w):
            acc = acc + ext_ref[pl.ds(HALO - k, tr), cols]
        cnt = jnp.clip(pos + 1, 1, w).astype(F32)
        d_ref[:, cols] = (acc / cnt - u).astype(d_ref.dtype)


def _pool_prompt(z_rest, pw, lp, pad):
    mp = z_rest.shape[0]
    tr = _pick_tile(lp, 384, BF16_SUBLANES)
    kern = functools.partial(_pool_prompt_kernel, tr=tr, lp=lp, pad=pad, pg=pw // len(POOL_WINDOWS))
    return pl.pallas_call(
        kern,
        grid=(mp // tr,),
        in_specs=[pl.BlockSpec((HALO, pw), lambda i: (jnp.maximum(i * (tr // HALO) - 1, 0), 0)),
                  pl.BlockSpec((tr, pw), lambda i: (i, 0))],
        out_specs=pl.BlockSpec((tr, pw), lambda i: (i, 0)),
        out_shape=jax.ShapeDtypeStruct((mp, pw), BF16),
        scratch_shapes=[pltpu.VMEM((tr + HALO, pw), F32)],
        compiler_params=_params("parallel"),
        name="pool_prompt",
    )(z_rest, z_rest)


def _pool_sample_kernel(st_ref, u_ref, d_ref, *, cnts, pg):
    u_all = u_ref[...]
    n_state = st_ref.shape[0]
    for g, cnt in enumerate(cnts):
        cols = slice(g * pg, (g + 1) * pg)
        u = u_all[:, cols]
        acc = u
        for k in range(1, cnt):
            acc = acc + st_ref[n_state - k, :, cols]
        d_ref[:, cols] = (acc / float(cnt) - u).astype(d_ref.dtype)


def _pool_sample(state_t, z_rest_s, pw, past_len):
    n_state, db, _ = state_t.shape
    tb = _pick_tile(db, 16, BF16_SUBLANES)
    cnts = tuple(min(past_len + 1, w) for w in POOL_WINDOWS)
    kern = functools.partial(_pool_sample_kernel, cnts=cnts, pg=pw // len(POOL_WINDOWS))
    return pl.pallas_call(
        kern,
        grid=(db // tb,),
        in_specs=[pl.BlockSpec((n_state, tb, pw), lambda i: (0, i, 0)),
                  pl.BlockSpec((tb, pw), lambda i: (i, 0))],
        out_specs=pl.BlockSpec((tb, pw), lambda i: (i, 0)),
        out_shape=jax.ShapeDtypeStruct((db, pw), BF16),
        compiler_params=_params("parallel"),
        name="pool_sample",
    )(state_t, z_rest_s)


def _poolmix_kernel(d_ref, w_ref, ga_ref, gp_ref, a_ref, ps_ref, o_ref):
    p = jnp.dot(d_ref[...], w_ref[0], preferred_element_type=F32) * ps_ref[...]
    o_ref[...] = (_sigmoid(ga_ref[...]) * a_ref[...] + _sigmoid(gp_ref[...]) * p).astype(o_ref.dtype)


def _poolmix(d, wpm, z_rest, a, pool_scale, pw):
    m, dm = a.shape
    ng, pg, pgo = wpm.shape
    tm = _pick_tile(m, 1056, BF16_SUBLANES)
    tn = pgo
    assert pw % tn == 0 and dm % tn == 0
    ga0 = pw // tn
    gp0 = (pw + dm) // tn
    return pl.pallas_call(
        _poolmix_kernel,
        grid=(m // tm, ng),
        in_specs=[pl.BlockSpec((tm, pg), lambda i, g: (i, g)),
                  pl.BlockSpec((1, pg, tn), lambda i, g: (g, 0, 0)),
                  pl.BlockSpec((tm, tn), lambda i, g: (i, ga0 + g)),
                  pl.BlockSpec((tm, tn), lambda i, g: (i, gp0 + g)),
                  pl.BlockSpec((tm, tn), lambda i, g: (i, g)),
                  pl.BlockSpec((1, tn), lambda i, g: (0, g))],
        out_specs=pl.BlockSpec((tm, tn), lambda i, g: (i, g)),
        out_shape=jax.ShapeDtypeStruct((m, dm), BF16),
        compiler_params=_params("parallel", "arbitrary"),
        name="poolmix",
    )(d, wpm, z_rest, z_rest, a, pool_scale)


def _ffn_tail(conv, b, wd_ref, o_ref):
    g = (_gelu_tanh(conv) * b).astype(BF16)
    contrib = jnp.dot(g, wd_ref[...], preferred_element_type=F32)
    j = pl.program_id(1)

    @pl.when(j == 0)
    def _():
        o_ref[...] = contrib

    @pl.when(j != 0)
    def _():
        o_ref[...] += contrib


def _ffn_prompt_kernel(h_ref, wgu_ref, wd_ref, cw_ref, cb_ref, o_ref, tail_ref, aext_ref, tails_ref,
                       *, tm, tf, lp, pad):
    i = pl.program_id(0)
    j = pl.program_id(1)
    ab = jnp.dot(h_ref[...], wgu_ref[0], preferred_element_type=F32)
    a = ab[:, :tf]
    b = ab[:, tf:]

    @pl.when(i == 0)
    def _():
        tails_ref[j] = jnp.zeros((CONV_TAIL, tf), F32)

    aext_ref[0:CONV_TAIL, :] = tails_ref[j]
    aext_ref[CONV_TAIL:, :] = a
    pos = lax.rem(i * tm, lp) - pad + lax.broadcasted_iota(jnp.int32, (tm, 1), 0)
    a1 = jnp.where(pos >= 1, aext_ref[pl.ds(CONV_TAIL - 1, tm), :], 0.0)
    a2 = jnp.where(pos >= 2, aext_ref[pl.ds(CONV_TAIL - 2, tm), :], 0.0)
    cw = cw_ref[...]
    conv = cb_ref[...] + cw[0:1] * a2 + cw[1:2] * a1 + cw[2:3] * a
    last = a[tm - CONV_TAIL:, :]
    tails_ref[j] = last
    tail_ref[0] = last
    _ffn_tail(conv, b, wd_ref, o_ref)


def _ffn_sample_kernel(h_ref, wgu_ref, wd_ref, cw_ref, cb_ref, s0_ref, s1_ref, o_ref, a_ref, *, tf):
    ab = jnp.dot(h_ref[...], wgu_ref[0], preferred_element_type=F32)
    a = ab[:, :tf]
    b = ab[:, tf:]
    a_ref[...] = a
    cw = cw_ref[...]
    conv = cb_ref[...] + cw[0:1] * s0_ref[...] + cw[1:2] * s1_ref[...] + cw[2:3] * a
    _ffn_tail(conv, b, wd_ref, o_ref)


def _ffn_specs(tm, dm, tf):
    return [pl.BlockSpec((tm, dm), lambda i, j: (i, 0)),
            pl.BlockSpec((1, dm, 2 * tf), lambda i, j: (j, 0, 0)),
            pl.BlockSpec((tf, dm), lambda i, j: (j, 0)),
            pl.BlockSpec((3, tf), lambda i, j: (0, j)),
            pl.BlockSpec((1, tf), lambda i, j: (0, j))]


def _ffn_prompt(h2, wgu, wd, cw, cb, lp, pad):
    mp, dm = h2.shape
    nj, _, tf2 = wgu.shape
    tf = tf2 // 2
    tm = _pick_tile(lp, 704, BF16_SUBLANES)
    kern = functools.partial(_ffn_prompt_kernel, tm=tm, tf=tf, lp=lp, pad=pad)
    return pl.pallas_call(
        kern,
        grid=(mp // tm, nj),
        in_specs=_ffn_specs(tm, dm, tf),
        out_specs=[pl.BlockSpec((tm, dm), lambda i, j: (i, 0), pipeline_mode=pl.Buffered(1)),
                   pl.BlockSpec((1, CONV_TAIL, tf), lambda i, j: (i, 0, j))],
        out_shape=[jax.ShapeDtypeStruct((mp, dm), F32),
                   jax.ShapeDtypeStruct((mp // tm, CONV_TAIL, nj * tf), F32)],
        scratch_shapes=[pltpu.VMEM((tm + CONV_TAIL, tf), F32), pltpu.VMEM((nj, CONV_TAIL, tf), F32)],
        compiler_params=_params("arbitrary", "arbitrary"),
        name="ffn_prompt",
    )(h2, wgu, wd, cw, cb)


def _ffn_sample(h2, wgu, wd, cw, cb, s0, s1):
    m, dm = h2.shape
    nj, _, tf2 = wgu.shape
    tf = tf2 // 2
    kern = functools.partial(_ffn_sample_kernel, tf=tf)
    state = pl.BlockSpec((m, tf), lambda i, j: (0, j))
    return pl.pallas_call(
        kern,
        grid=(1, nj),
        in_specs=_ffn_specs(m, dm, tf) + [state, state],
        out_specs=[pl.BlockSpec((m, dm), lambda i, j: (0, 0)), pl.BlockSpec((m, tf), lambda i, j: (0, j))],
        out_shape=[jax.ShapeDtypeStruct((m, dm), F32), jax.ShapeDtypeStruct((m, nj * tf), F32)],
        compiler_params=_params("arbitrary", "arbitrary"),
        name="ffn_sample",
    )(h2, wgu, wd, cw, cb, s0, s1)


def kernel(x_prompt, x_sample, cache_k, cache_v, cache_logf, state_pool, state_conv, page_table, meta_tokens, g_pre_mix, g_post_mix, g_pre_ffn, g_post_ffn, w_in, b_forget, w_head_out, w_pool_map, pool_scale, w_out, w_gate, w_up, conv_w, conv_b, w_down):
    nb, seq, dm = x_prompt.shape
    db = x_sample.shape[0]
    depth, _, ps, nh, dh = cache_k.shape
    assert depth == 1 and x_sample.shape[1] == 1
    n_meta = meta_tokens.shape[0]
    n_state, pw = state_pool.shape[2], state_pool.shape[3]
    dff = w_gate.shape[2]
    aw = nh * dh
    past_len = page_table.shape[1] * ps
    assert n_state <= HALO - 1 and w_in.shape[2] == 3 * aw + nh + pw + 2 * dm

    pad = (-(n_meta + seq)) % LANES
    if pad < HALO:
        pad += LANES
    lp = pad + n_meta + seq
    mp = nb * lp

    w_in0 = w_in[0]
    w_qkv = w_in0[:, :3 * aw].astype(BF16)
    w_f = w_in0[:, 3 * aw:3 * aw + nh]
    w_rest = w_in0[:, 3 * aw + nh:].astype(BF16)
    wf_t = w_f.T.astype(BF16)
    wf_pad = jnp.pad(w_f, ((0, 0), (0, LANES - nh))).astype(BF16)
    who = w_head_out[0].astype(BF16)
    wpm = w_pool_map[0].astype(BF16)
    wo = w_out[0].astype(BF16)
    tf = _pick_tile(dff, 256, LANES)
    nj = dff // tf
    wgu = jnp.concatenate([w_gate[0].reshape(dm, nj, tf), w_up[0].reshape(dm, nj, tf)], axis=2)
    wgu = wgu.transpose(1, 0, 2).astype(BF16)
    wd = w_down[0].astype(BF16)
    cw = conv_w[0]
    cb = conv_b[0].reshape(1, dff)
    g1, g2, g3, g4 = (g[0].reshape(1, dm) for g in (g_pre_mix, g_post_mix, g_pre_ffn, g_post_ffn))
    pscale = pool_scale[0].reshape(1, dm)
    bf_col = b_forget[0].reshape(nh, 1)
    bf_row = b_forget[0].reshape(1, nh)

    meta = jnp.broadcast_to(meta_tokens[None], (nb, n_meta, dm))
    xp = jnp.concatenate([jnp.zeros((nb, pad, dm), F32), meta, x_prompt], axis=1).reshape(mp, dm)
    h1 = _rmsnorm(xp, g1, BF16)
    zqkv = _matmul(h1, w_qkv)
    z_rest = _matmul(h1, w_rest)
    lf_t, c_t = _forget_prompt(h1, wf_t, bf_col, lp, pad)
    a = _attn_prompt(zqkv, c_t, who, nb, lp, nh, dh)
    d = _pool_prompt(z_rest, pw, lp, pad)
    mix = _poolmix(d, wpm, z_rest, a, pscale, pw)
    mm = _matmul(mix, wo)
    x2, h2 = _post_mix(mm, xp, g2, g3)
    f, tails = _ffn_prompt(h2, wgu, wd, cw, cb, lp, pad)
    y = _post_ffn(f, x2, g4)

    first = pad + n_meta
    y_prompt = y.reshape(nb, lp, dm)[:, first:]
    k_prompt = zqkv[:, aw:2 * aw].reshape(nb, lp, nh, dh)[:, pad:][None]
    v_prompt = zqkv[:, 2 * aw:].reshape(nb, lp, nh, dh)[:, pad:][None]
    logf_prompt = lf_t.T.reshape(nb, lp, nh)[:, pad:][None]
    pool_prompt = z_rest[:, :pw].reshape(nb, lp, pw)[:, lp - n_state:][None]
    tiles_per_seq = tails.shape[0] // nb
    conv_prompt = tails.reshape(nb, tiles_per_seq, CONV_TAIL, dff)[:, -1, CONV_TAIL - 2:][None]

    xs = x_sample.reshape(db, dm)
    h1s = _rmsnorm(xs, g1, BF16)
    zqkv_s = _matmul(h1s, w_qkv)
    z_rest_s = _matmul(h1s, w_rest)
    fl_s = _matmul(h1s, wf_pad)
    o_s, lf_s = _attn_sample(page_table, zqkv_s, fl_s, bf_row, cache_k[0], cache_v[0], cache_logf[0], nh, dh)
    a_s = _head_out(o_s, who)
    d_s = _pool_sample(state_pool[0].transpose(1, 0, 2), z_rest_s, pw, past_len)
    mix_s = _poolmix(d_s, wpm, z_rest_s, a_s, pscale, pw)
    mm_s = _matmul(mix_s, wo)
    x2s, h2s = _post_mix(mm_s, xs, g2, g3)
    f_s, a_new = _ffn_sample(h2s, wgu, wd, cw, cb, state_conv[0][:, 0], state_conv[0][:, 1])
    y_s = _post_ffn(f_s, x2s, g4)

    y_sample = y_s.reshape(db, 1, dm)
    k_sample = zqkv_s[:, aw:2 * aw].reshape(1, db, 1, nh, dh)
    v_sample = zqkv_s[:, 2 * aw:].reshape(1, db, 1, nh, dh)
    logf_sample = lf_s.reshape(1, db, 1, nh)
    u_s = z_rest_s[:, :pw]
    pool_sample = jnp.concatenate([state_pool[0][:, 1:], u_s[:, None]], axis=1)[None]
    conv_sample = jnp.stack([state_conv[0][:, 1], a_new], axis=1)[None]

    return (y_prompt, y_sample, k_prompt, v_prompt, logf_prompt, pool_prompt, conv_prompt,
            k_sample, v_sample, logf_sample, pool_sample, conv_sample)
```

```python
import functools
import math

import jax
import jax.numpy as jnp
from jax import lax
from jax.experimental import pallas as pl
from jax.experimental.pallas import tpu as pltpu

F32 = jnp.float32
BF16 = jnp.bfloat16
RMS_EPS = 1e-6
POOL_WINDOWS = (2, 4, 8, 16)
LANES = 128
BF16_SUBLANES = 16
HALO = 16
CONV_TAIL = 8
MASKED_CUMLOG = 1e30
VMEM_LIMIT = 56 * 1024 * 1024


def _pick_tile(n, cap, mult):
    best = None
    for t in range(mult, min(n, cap) + 1, mult):
        if n % t == 0:
            best = t
    return n if best is None else best


def _params(*sem):
    return pltpu.CompilerParams(dimension_semantics=sem, vmem_limit_bytes=VMEM_LIMIT)


def _rms(x, g):
    return x * lax.rsqrt(jnp.mean(x * x, axis=-1, keepdims=True) + RMS_EPS) * g


def _sigmoid(x):
    return 1.0 / (1.0 + jnp.exp(-x))


def _log_sigmoid(x):
    return jnp.minimum(x, 0.0) - jnp.log(1.0 + jnp.exp(-jnp.abs(x)))


def _gelu_tanh(x):
    cdf = 0.5 * (1.0 + jnp.tanh(math.sqrt(2.0 / math.pi) * (x + 0.044715 * (x * x * x))))
    return x * cdf


def _split3(x):
    hi = x.astype(BF16)
    r = x - hi.astype(F32)
    mid = r.astype(BF16)
    lo = (r - mid.astype(F32)).astype(BF16)
    return hi, mid, lo


_NN = (((1,), (0,)), ((), ()))
_NT = (((1,), (1,)), ((), ()))


def _dot_f32_lhs(x, m_bf, dims):
    out = None
    for piece in _split3(x):
        t = lax.dot_general(piece, m_bf, dims, preferred_element_type=F32)
        out = t if out is None else out + t
    return out


def _dot_f32_rhs(m_bf, x, dims):
    out = None
    for piece in _split3(x):
        t = lax.dot_general(m_bf, piece, dims, preferred_element_type=F32)
        out = t if out is None else out + t
    return out


def _rmsnorm_kernel(x_ref, g_ref, o_ref):
    o_ref[...] = _rms(x_ref[...], g_ref[...]).astype(o_ref.dtype)


def _rmsnorm(x, g, out_dtype):
    m, d = x.shape
    tr = _pick_tile(m, 256, BF16_SUBLANES)
    return pl.pallas_call(
        _rmsnorm_kernel,
        grid=(m // tr,),
        in_specs=[pl.BlockSpec((tr, d), lambda i: (i, 0)), pl.BlockSpec((1, d), lambda i: (0, 0))],
        out_specs=pl.BlockSpec((tr, d), lambda i: (i, 0)),
        out_shape=jax.ShapeDtypeStruct((m, d), out_dtype),
        compiler_params=_params("parallel"),
        name="rmsnorm",
    )(x, g)


def _post_mix_kernel(m_ref, x_ref, g1_ref, g2_ref, x2_ref, h2_ref):
    x2 = x_ref[...] + _rms(m_ref[...], g1_ref[...])
    x2_ref[...] = x2
    h2_ref[...] = _rms(x2, g2_ref[...]).astype(h2_ref.dtype)


def _post_mix(mm, x, g1, g2):
    m, d = x.shape
    tr = _pick_tile(m, 256, BF16_SUBLANES)
    row = pl.BlockSpec((tr, d), lambda i: (i, 0))
    vec = pl.BlockSpec((1, d), lambda i: (0, 0))
    return pl.pallas_call(
        _post_mix_kernel,
        grid=(m // tr,),
        in_specs=[row, row, vec, vec],
        out_specs=[row, row],
        out_shape=[jax.ShapeDtypeStruct((m, d), F32), jax.ShapeDtypeStruct((m, d), BF16)],
        compiler_params=_params("parallel"),
        name="post_mix",
    )(mm, x, g1, g2)


def _post_ffn_kernel(f_ref, x_ref, g_ref, y_ref):
    y_ref[...] = x_ref[...] + _rms(f_ref[...], g_ref[...])


def _post_ffn(f, x, g):
    m, d = x.shape
    tr = _pick_tile(m, 256, 8)
    row = pl.BlockSpec((tr, d), lambda i: (i, 0))
    vec = pl.BlockSpec((1, d), lambda i: (0, 0))
    return pl.pallas_call(
        _post_ffn_kernel,
        grid=(m // tr,),
        in_specs=[row, row, vec],
        out_specs=row,
        out_shape=jax.ShapeDtypeStruct((m, d), F32),
        compiler_params=_params("parallel"),
        name="post_ffn",
    )(f, x, g)


def _mm_kernel(x_ref, w_ref, o_ref):
    o_ref[...] = jnp.dot(x_ref[...], w_ref[...], preferred_element_type=F32).astype(o_ref.dtype)


def _matmul(x, w, out_dtype=F32, tm_cap=1408, tn_cap=512, x_buffers=2, name="matmul"):
    m, k = x.shape
    n = w.shape[1]
    tm = _pick_tile(m, tm_cap, BF16_SUBLANES)
    tn = _pick_tile(n, tn_cap, LANES)
    x_mode = {} if x_buffers == 2 else {"pipeline_mode": pl.Buffered(x_buffers)}
    x_spec = pl.BlockSpec((tm, k), lambda i, j: (i, 0), **x_mode)
    return pl.pallas_call(
        _mm_kernel,
        grid=(m // tm, n // tn),
        in_specs=[x_spec, pl.BlockSpec((k, tn), lambda i, j: (0, j))],
        out_specs=pl.BlockSpec((tm, tn), lambda i, j: (i, j)),
        out_shape=jax.ShapeDtypeStruct((m, n), out_dtype),
        compiler_params=_params("parallel", "arbitrary"),
        name=name,
    )(x, w)


def _forget_kernel(h_ref, wf_ref, b_ref, lf_ref, c_ref, carry_ref, *, tc, lp, pad):
    i = pl.program_id(0)
    off = lax.rem(i * tc, lp)
    fl = lax.dot_general(wf_ref[...], h_ref[...], _NT, preferred_element_type=F32)
    lf = _log_sigmoid(fl + b_ref[...])
    pos = off + lax.broadcasted_iota(jnp.int32, (1, tc), 1)
    valid = pos >= pad
    lfm = jnp.where(valid, lf, 0.0)
    upper = (lax.broadcasted_iota(jnp.int32, (tc, tc), 0)
             <= lax.broadcasted_iota(jnp.int32, (tc, tc), 1)).astype(BF16)
    cs = _dot_f32_lhs(lfm, upper, _NN)

    @pl.when(off == 0)
    def _():
        carry_ref[...] = jnp.zeros_like(carry_ref)

    c = cs + carry_ref[:, 0:1]
    carry_ref[...] = jnp.broadcast_to(c[:, tc - 1:tc], carry_ref.shape)
    lf_ref[...] = lf
    c_ref[...] = jnp.where(valid, c, MASKED_CUMLOG)


def _forget_prompt(h, wf_t, b_col, lp, pad):
    mp, d = h.shape
    nh = wf_t.shape[0]
    tc = _pick_tile(lp, 384, LANES)
    kern = functools.partial(_forget_kernel, tc=tc, lp=lp, pad=pad)
    return pl.pallas_call(
        kern,
        grid=(mp // tc,),
        in_specs=[pl.BlockSpec((tc, d), lambda i: (i, 0)),
                  pl.BlockSpec((nh, d), lambda i: (0, 0)),
                  pl.BlockSpec((nh, 1), lambda i: (0, 0))],
        out_specs=[pl.BlockSpec((nh, tc), lambda i: (0, i)), pl.BlockSpec((nh, tc), lambda i: (0, i))],
        out_shape=[jax.ShapeDtypeStruct((nh, mp), F32), jax.ShapeDtypeStruct((nh, mp), F32)],
        scratch_shapes=[pltpu.VMEM((nh, LANES), F32)],
        compiler_params=_params("arbitrary"),
        name="forget_prompt",
    )(h, wf_t, b_col)


def _attn_kernel(q_ref, k_ref, v_ref, c_ref, who_ref, a_ref, kb_ref, vb_ref, *, t, scale):
    iq = pl.program_id(2)

    @pl.when(iq == 0)
    def _():
        kb_ref[...] = k_ref[...].astype(BF16)
        vb_ref[...] = v_ref[...].astype(BF16)

    q = q_ref[...].astype(BF16)
    dh = q.shape[1]
    rows = iq * t + lax.broadcasted_iota(jnp.int32, (t, 1), 0)

    def step(j, carry, diagonal):
        m, l, acc = carry
        start = pl.multiple_of(j * t, t)
        ks = kb_ref[pl.ds(start, t), :]
        vs = vb_ref[pl.ds(start, t), :]
        s = lax.dot_general(q, ks, _NT, preferred_element_type=F32)
        tt = s * scale - c_ref[0, j]
        if diagonal:
            cols = j * t + lax.broadcasted_iota(jnp.int32, (1, t), 1)
            tt = jnp.where(cols <= rows, tt, -jnp.inf)
        m_new = jnp.maximum(m, jnp.max(tt, axis=-1, keepdims=True))
        alpha = jnp.exp(m - m_new)
        p = jnp.exp(tt - m_new)
        l = alpha * l + jnp.sum(p, axis=-1, keepdims=True)
        acc = alpha * acc + jnp.dot(p.astype(BF16), vs, preferred_element_type=F32)
        return m_new, l, acc

    init = (jnp.full((t, 1), -jnp.inf, F32), jnp.zeros((t, 1), F32), jnp.zeros((t, dh), F32))
    carry = lax.fori_loop(0, iq, functools.partial(step, diagonal=False), init)
    _, l, acc = step(iq, carry, True)
    o = (acc / l).astype(BF16)
    a_ref[...] = jnp.dot(o, who_ref[0], preferred_element_type=F32)


def _attn_prompt(zqkv, c_t, who, nb, lp, nh, dh):
    mp = zqkv.shape[0]
    ho = who.shape[2]
    t = _pick_tile(lp, 384, LANES)
    nq = lp // t
    c4 = c_t.reshape(nh, mp // t, 1, t)
    kern = functools.partial(_attn_kernel, t=t, scale=dh ** -0.5)
    return pl.pallas_call(
        kern,
        grid=(nb, nh, nq),
        in_specs=[pl.BlockSpec((t, dh), lambda b, h, i: (b * nq + i, h)),
                  pl.BlockSpec((lp, dh), lambda b, h, i: (b, nh + h)),
                  pl.BlockSpec((lp, dh), lambda b, h, i: (b, 2 * nh + h)),
                  pl.BlockSpec((1, nq, 1, t), lambda b, h, i: (h, b, 0, 0)),
                  pl.BlockSpec((1, dh, ho), lambda b, h, i: (h, 0, 0))],
        out_specs=pl.BlockSpec((t, ho), lambda b, h, i: (b * nq + i, h)),
        out_shape=jax.ShapeDtypeStruct((mp, nh * ho), F32),
        scratch_shapes=[pltpu.VMEM((lp, dh), BF16), pltpu.VMEM((lp, dh), BF16)],
        compiler_params=_params("parallel", "parallel", "arbitrary"),
        name="attn_prompt",
    )(zqkv, zqkv, zqkv, c4, who)


def _softmax_update(tiles, m_ref, l_ref, acc_ref, pv):
    mt = tiles[0]
    for t in tiles[1:]:
        mt = jnp.maximum(mt, t)
    m_old = m_ref[:, 0:1]
    m_new = jnp.maximum(m_old, jnp.max(mt, axis=1, keepdims=True))
    alpha = jnp.exp(m_old - m_new)
    ps = [jnp.exp(t - m_new) for t in tiles]
    tot = ps[0]
    for p in ps[1:]:
        tot = tot + p
    l_ref[...] = jnp.broadcast_to(alpha * l_ref[:, 0:1] + jnp.sum(tot, axis=1, keepdims=True), l_ref.shape)
    acc_ref[...] = alpha * acc_ref[...] + pv(ps)
    m_ref[...] = jnp.broadcast_to(m_new, m_ref.shape)


def _sattn_kernel(pt_ref, q_ref, kn_ref, vn_ref, fl_ref, bf_ref, w2_ref, *refs, nh, dh, ps, gp, scale):
    del pt_ref
    k_refs, v_refs, lf_refs = refs[0:gp], refs[gp:2 * gp], refs[2 * gp:3 * gp]
    o_ref, lfo_ref = refs[3 * gp], refs[3 * gp + 1]
    m_ref, l_ref, carry_ref, acc_ref, lfs_ref = refs[3 * gp + 2:]
    step = pl.program_id(1)
    rows = ps * nh
    nchunk = rows // LANES
    lane = lax.broadcasted_iota(jnp.int32, (nh, LANES), 1)
    diag = (lane & (nh - 1)) == lax.broadcasted_iota(jnp.int32, (nh, LANES), 0)

    @pl.when(step == 0)
    def _():
        m_ref[...] = jnp.full_like(m_ref, -jnp.inf)
        l_ref[...] = jnp.zeros_like(l_ref)
        carry_ref[...] = jnp.zeros_like(carry_ref)
        acc_ref[...] = jnp.zeros_like(acc_ref)

    qb = q_ref[0].astype(BF16)

    for g in range(gp):
        lfs_ref[g:g + 1, :] = lf_refs[g][0]
    x = lfs_ref[...]
    w2 = w2_ref[...]
    run = jnp.zeros((gp, LANES), F32)
    cks = []
    for k in range(nchunk):
        r = _dot_f32_lhs(x[:, k * LANES:(k + 1) * LANES], w2, _NN)
        cks.append(r[:, :LANES] + run)
        run = run + r[:, LANES:]
    earlier = (lax.broadcasted_iota(jnp.int32, (gp, gp), 0)
               > lax.broadcasted_iota(jnp.int32, (gp, gp), 1)).astype(BF16)
    offs = _dot_f32_rhs(earlier, run, _NN) + carry_ref[0:1, :]
    carry_new = offs[gp - 1:gp] + run[gp - 1:gp]

    tiles = []
    for g in range(gp):
        kf = k_refs[g][0, 0].reshape(rows, dh).astype(BF16)
        zt = lax.dot_general(qb, kf, _NT, preferred_element_type=F32)
        for k in range(nchunk):
            c_row = cks[k][g:g + 1] + offs[g:g + 1]
            tiles.append(jnp.where(diag, zt[:, k * LANES:(k + 1) * LANES] * scale - c_row, -jnp.inf))

    def pv(probs):
        out = None
        for g in range(gp):
            pm = jnp.concatenate(probs[g * nchunk:(g + 1) * nchunk], axis=1).astype(BF16)
            vf = v_refs[g][0, 0].reshape(rows, dh).astype(BF16)
            o = jnp.dot(pm, vf, preferred_element_type=F32)
            out = o if out is None else out + o
        return out

    _softmax_update(tiles, m_ref, l_ref, acc_ref, pv)
    carry_ref[...] = jnp.broadcast_to(carry_new, carry_ref.shape)

    @pl.when(step == pl.num_programs(1) - 1)
    def _():
        lf_new = _log_sigmoid(fl_ref[0] + bf_ref[...])
        lfo_ref[0] = lf_new
        c_new = carry_new + lf_new
        zeros = jnp.zeros((LANES - nh, dh), F32)
        kpad = jnp.concatenate([kn_ref[0], zeros], axis=0).astype(BF16)
        vpad = jnp.concatenate([vn_ref[0], zeros], axis=0).astype(BF16)
        zt = lax.dot_general(qb, kpad, _NT, preferred_element_type=F32)
        tile = jnp.where(diag & (lane < nh), zt * scale - c_new, -jnp.inf)
        _softmax_update([tile], m_ref, l_ref, acc_ref,
                        lambda probs: jnp.dot(probs[0].astype(BF16), vpad, preferred_element_type=F32))
        o_ref[0] = acc_ref[...] / l_ref[:, 0:1]


def _attn_sample(page_table, zqkv_s, fl_s, bf_pad, cache_k, cache_v, cache_lf, nh, dh):
    db, n_pages = page_table.shape
    n_phys, ps = cache_k.shape[1], cache_k.shape[2]
    assert nh & (nh - 1) == 0 and LANES % nh == 0 and nh % 8 == 0
    rows = ps * nh
    gp = _pick_tile(n_pages, 8, 1)
    qkv_h = zqkv_s.reshape(db, 3 * nh, dh)
    fl3 = fl_s.reshape(db, 1, LANES)
    lf_flat = cache_lf.reshape(n_phys, 1, rows)
    r = jnp.arange(LANES)
    same = (r[:, None] % nh) == (r[None, :] % nh)
    w2 = jnp.concatenate([same & (r[:, None] <= r[None, :]), same], axis=1).astype(BF16)
    kern = functools.partial(_sattn_kernel, nh=nh, dh=dh, ps=ps, gp=gp, scale=dh ** -0.5)

    def page_spec(g):
        return pl.BlockSpec((1, 1, ps, nh, dh), lambda b, s, pt: (0, pt[b, s * gp + g], 0, 0, 0))

    def lf_spec(g):
        return pl.BlockSpec((1, 1, rows), lambda b, s, pt: (pt[b, s * gp + g], 0, 0))

    grid_spec = pltpu.PrefetchScalarGridSpec(
        num_scalar_prefetch=1,
        grid=(db, n_pages // gp),
        in_specs=[pl.BlockSpec((1, nh, dh), lambda b, s, pt: (b, 0, 0)),
                  pl.BlockSpec((1, nh, dh), lambda b, s, pt: (b, 1, 0)),
                  pl.BlockSpec((1, nh, dh), lambda b, s, pt: (b, 2, 0)),
                  pl.BlockSpec((1, 1, LANES), lambda b, s, pt: (b, 0, 0)),
                  pl.BlockSpec((1, LANES), lambda b, s, pt: (0, 0)),
                  pl.BlockSpec((LANES, 2 * LANES), lambda b, s, pt: (0, 0))]
                 + [page_spec(g) for g in range(gp)] + [page_spec(g) for g in range(gp)]
                 + [lf_spec(g) for g in range(gp)],
        out_specs=[pl.BlockSpec((1, nh, dh), lambda b, s, pt: (b, 0, 0)),
                   pl.BlockSpec((1, 1, LANES), lambda b, s, pt: (b, 0, 0))],
        scratch_shapes=[pltpu.VMEM((nh, LANES), F32), pltpu.VMEM((nh, LANES), F32),
                        pltpu.VMEM((8, LANES), F32), pltpu.VMEM((nh, dh), F32),
                        pltpu.VMEM((gp, rows), F32)],
    )
    o3, lf3 = pl.pallas_call(
        kern,
        grid_spec=grid_spec,
        out_shape=[jax.ShapeDtypeStruct((db, nh, dh), F32), jax.ShapeDtypeStruct((db, 1, LANES), F32)],
        compiler_params=_params("parallel", "arbitrary"),
        name="attn_sample",
    )(page_table, qkv_h, qkv_h, qkv_h, fl3, bf_pad, w2,
      *([cache_k] * gp), *([cache_v] * gp), *([lf_flat] * gp))
    return o3.reshape(db, nh * dh), lf3.reshape(db, LANES)[:, :nh]


def _head_out_kernel(o_ref, w_ref, a_ref):
    a_ref[...] = jnp.dot(o_ref[...].astype(BF16), w_ref[0], preferred_element_type=F32)


def _head_out(o, who):
    m = o.shape[0]
    nh, dh, ho = who.shape
    return pl.pallas_call(
        _head_out_kernel,
        grid=(nh,),
        in_specs=[pl.BlockSpec((m, dh), lambda h: (0, h)), pl.BlockSpec((1, dh, ho), lambda h: (h, 0, 0))],
        out_specs=pl.BlockSpec((m, ho), lambda h: (0, h)),
        out_shape=jax.ShapeDtypeStruct((m, nh * ho), F32),
        compiler_params=_params("parallel"),
        name="head_out",
    )(o, who)


def _pool_prompt_kernel(halo_ref, u_ref, d_ref, ext_ref, *, tr, lp, pad, pg):
    i = pl.program_id(0)
    ext_ref[0:HALO, :] = halo_ref[...]
    ext_ref[HALO:, :] = u_ref[...]
    pos = lax.rem(i * tr, lp) - pad + lax.broadcasted_iota(jnp.int32, (tr, 1), 0)
    for g, w in enumerate(POOL_WINDOWS):
        cols = slice(g * pg, (g + 1) * pg)
        u = ext_ref[pl.ds(HALO, tr), cols]
        acc = u
        for k in range(1, w):
            acc = acc + ext_ref[pl.ds(HALO - k, tr), cols]
        cnt = jnp.clip(pos + 1, 1, w).astype(F32)
        d_ref[:, cols] = (acc / cnt - u).astype(d_ref.dtype)


def _pool_prompt(z_rest, pw, lp, pad):
    mp = z_rest.shape[0]
    tr = _pick_tile(lp, 384, BF16_SUBLANES)
    kern = functools.partial(_pool_prompt_kernel, tr=tr, lp=lp, pad=pad, pg=pw // len(POOL_WINDOWS))
    return pl.pallas_call(
        kern,
        grid=(mp // tr,),
        in_specs=[pl.BlockSpec((HALO, pw), lambda i: (jnp.maximum(i * (tr // HALO) - 1, 0), 0)),
                  pl.BlockSpec((tr, pw), lambda i: (i, 0))],
        out_specs=pl.BlockSpec((tr, pw), lambda i: (i, 0)),
        out_shape=jax.ShapeDtypeStruct((mp, pw), BF16),
        scratch_shapes=[pltpu.VMEM((tr + HALO, pw), F32)],
        compiler_params=_params("parallel"),
        name="pool_prompt",
    )(z_rest, z_rest)


def _pool_sample_kernel(st_ref, u_ref, d_ref, *, cnts, pg):
    u_all = u_ref[...]
    n_state = st_ref.shape[0]
    for g, cnt in enumerate(cnts):
        cols = slice(g * pg, (g + 1) * pg)
        u = u_all[:, cols]
        acc = u
        for k in range(1, cnt):
            acc = acc + st_ref[n_state - k, :, cols]
        d_ref[:, cols] = (acc / float(cnt) - u).astype(d_ref.dtype)


def _pool_sample(state_t, z_rest_s, pw, past_len):
    n_state, db, _ = state_t.shape
    tb = _pick_tile(db, 16, BF16_SUBLANES)
    cnts = tuple(min(past_len + 1, w) for w in POOL_WINDOWS)
    kern = functools.partial(_pool_sample_kernel, cnts=cnts, pg=pw // len(POOL_WINDOWS))
    return pl.pallas_call(
        kern,
        grid=(db // tb,),
        in_specs=[pl.BlockSpec((n_state, tb, pw), lambda i: (0, i, 0)),
                  pl.BlockSpec((tb, pw), lambda i: (i, 0))],
        out_specs=pl.BlockSpec((tb, pw), lambda i: (i, 0)),
        out_shape=jax.ShapeDtypeStruct((db, pw), BF16),
        compiler_params=_params("parallel"),
        name="pool_sample",
    )(state_t, z_rest_s)


def _poolmix_kernel(d_ref, w_ref, ga_ref, gp_ref, a_ref, ps_ref, o_ref):
    p = jnp.dot(d_ref[...], w_ref[0], preferred_element_type=F32) * ps_ref[...]
    o_ref[...] = (_sigmoid(ga_ref[...]) * a_ref[...] + _sigmoid(gp_ref[...]) * p).astype(o_ref.dtype)


def _poolmix(d, wpm, z_rest, a, pool_scale, pw):
    m, dm = a.shape
    ng, pg, pgo = wpm.shape
    tm = _pick_tile(m, 1056, BF16_SUBLANES)
    tn = pgo
    assert pw % tn == 0 and dm % tn == 0
    ga0 = pw // tn
    gp0 = (pw + dm) // tn
    return pl.pallas_call(
        _poolmix_kernel,
        grid=(m // tm, ng),
        in_specs=[pl.BlockSpec((tm, pg), lambda i, g: (i, g)),
                  pl.BlockSpec((1, pg, tn), lambda i, g: (g, 0, 0)),
                  pl.BlockSpec((tm, tn), lambda i, g: (i, ga0 + g)),
                  pl.BlockSpec((tm, tn), lambda i, g: (i, gp0 + g)),
                  pl.BlockSpec((tm, tn), lambda i, g: (i, g)),
                  pl.BlockSpec((1, tn), lambda i, g: (0, g))],
        out_specs=pl.BlockSpec((tm, tn), lambda i, g: (i, g)),
        out_shape=jax.ShapeDtypeStruct((m, dm), BF16),
        compiler_params=_params("parallel", "arbitrary"),
        name="poolmix",
    )(d, wpm, z_rest, z_rest, a, pool_scale)


def _conv_gate(a, a1, a2, b, cw_ref, cb_ref):
    cw = cw_ref[...]
    conv = cb_ref[...] + cw[0:1] * a2 + cw[1:2] * a1 + cw[2:3] * a
    return (_gelu_tanh(conv) * b).astype(BF16)


def _ffn_gu_prompt_kernel(h_ref, wgu_ref, cw_ref, cb_ref, g_ref, tail_ref, aext_ref, tails_ref,
                          *, tm, tf, lp, pad):
    i = pl.program_id(0)
    j = pl.program_id(1)
    ab = jnp.dot(h_ref[...], wgu_ref[0], preferred_element_type=F32)
    a = ab[:, :tf]
    b = ab[:, tf:]

    @pl.when(i == 0)
    def _():
        tails_ref[j] = jnp.zeros((CONV_TAIL, tf), F32)

    aext_ref[0:CONV_TAIL, :] = tails_ref[j]
    aext_ref[CONV_TAIL:, :] = a
    pos = lax.rem(i * tm, lp) - pad + lax.broadcasted_iota(jnp.int32, (tm, 1), 0)
    a1 = jnp.where(pos >= 1, aext_ref[pl.ds(CONV_TAIL - 1, tm), :], 0.0)
    a2 = jnp.where(pos >= 2, aext_ref[pl.ds(CONV_TAIL - 2, tm), :], 0.0)
    last = a[tm - CONV_TAIL:, :]
    tails_ref[j] = last
    tail_ref[0] = last
    g_ref[...] = _conv_gate(a, a1, a2, b, cw_ref, cb_ref)


def _ffn_gu_sample_kernel(h_ref, wgu_ref, cw_ref, cb_ref, s0_ref, s1_ref, g_ref, a_ref, *, tf):
    ab = jnp.dot(h_ref[...], wgu_ref[0], preferred_element_type=F32)
    a = ab[:, :tf]
    a_ref[...] = a
    g_ref[...] = _conv_gate(a, s1_ref[...], s0_ref[...], ab[:, tf:], cw_ref, cb_ref)


def _ffn_gu_specs(tm, dm, tf):
    return [pl.BlockSpec((tm, dm), lambda i, j: (i, 0)),
            pl.BlockSpec((1, dm, 2 * tf), lambda i, j: (j, 0, 0)),
            pl.BlockSpec((3, tf), lambda i, j: (0, j)),
            pl.BlockSpec((1, tf), lambda i, j: (0, j))]


def _ffn_gu_prompt(h2, wgu, cw, cb, lp, pad):
    mp, dm = h2.shape
    nj, _, tf2 = wgu.shape
    tf = tf2 // 2
    tm = _pick_tile(lp, 1408, BF16_SUBLANES)
    kern = functools.partial(_ffn_gu_prompt_kernel, tm=tm, tf=tf, lp=lp, pad=pad)
    return pl.pallas_call(
        kern,
        grid=(mp // tm, nj),
        in_specs=_ffn_gu_specs(tm, dm, tf),
        out_specs=[pl.BlockSpec((tm, tf), lambda i, j: (i, j)),
                   pl.BlockSpec((1, CONV_TAIL, tf), lambda i, j: (i, 0, j))],
        out_shape=[jax.ShapeDtypeStruct((mp, nj * tf), BF16),
                   jax.ShapeDtypeStruct((mp // tm, CONV_TAIL, nj * tf), F32)],
        scratch_shapes=[pltpu.VMEM((tm + CONV_TAIL, tf), F32), pltpu.VMEM((nj, CONV_TAIL, tf), F32)],
        compiler_params=_params("arbitrary", "arbitrary"),
        name="ffn_gu_prompt",
    )(h2, wgu, cw, cb)


def _ffn_gu_sample(h2, wgu, cw, cb, s0, s1):
    m, dm = h2.shape
    nj, _, tf2 = wgu.shape
    tf = tf2 // 2
    kern = functools.partial(_ffn_gu_sample_kernel, tf=tf)
    col = pl.BlockSpec((m, tf), lambda i, j: (0, j))
    return pl.pallas_call(
        kern,
        grid=(1, nj),
        in_specs=_ffn_gu_specs(m, dm, tf) + [col, col],
        out_specs=[col, col],
        out_shape=[jax.ShapeDtypeStruct((m, nj * tf), BF16), jax.ShapeDtypeStruct((m, nj * tf), F32)],
        compiler_params=_params("arbitrary", "arbitrary"),
        name="ffn_gu_sample",
    )(h2, wgu, cw, cb, s0, s1)


def kernel(x_prompt, x_sample, cache_k, cache_v, cache_logf, state_pool, state_conv, page_table, meta_tokens, g_pre_mix, g_post_mix, g_pre_ffn, g_post_ffn, w_in, b_forget, w_head_out, w_pool_map, pool_scale, w_out, w_gate, w_up, conv_w, conv_b, w_down):
    nb, seq, dm = x_prompt.shape
    db = x_sample.shape[0]
    depth, _, ps, nh, dh = cache_k.shape
    assert depth == 1 and x_sample.shape[1] == 1
    n_meta = meta_tokens.shape[0]
    n_state, pw = state_pool.shape[2], state_pool.shape[3]
    dff = w_gate.shape[2]
    aw = nh * dh
    past_len = page_table.shape[1] * ps
    assert n_state <= HALO - 1 and w_in.shape[2] == 3 * aw + nh + pw + 2 * dm

    pad = (-(n_meta + seq)) % LANES
    if pad < HALO:
        pad += LANES
    lp = pad + n_meta + seq
    mp = nb * lp

    w_in0 = w_in[0]
    w_qkv = w_in0[:, :3 * aw].astype(BF16)
    w_f = w_in0[:, 3 * aw:3 * aw + nh]
    w_rest = w_in0[:, 3 * aw + nh:].astype(BF16)
    wf_t = w_f.T.astype(BF16)
    wf_pad = jnp.pad(w_f, ((0, 0), (0, LANES - nh))).astype(BF16)
    who = w_head_out[0].astype(BF16)
    wpm = w_pool_map[0].astype(BF16)
    wo = w_out[0].astype(BF16)
    tf = _pick_tile(dff, 256, LANES)
    nj = dff // tf
    wgu = jnp.concatenate([w_gate[0].reshape(dm, nj, tf), w_up[0].reshape(dm, nj, tf)], axis=2)
    wgu = wgu.transpose(1, 0, 2).astype(BF16)
    wd = w_down[0].astype(BF16)
    cw = conv_w[0]
    cb = conv_b[0].reshape(1, dff)
    g1, g2, g3, g4 = (g[0].reshape(1, dm) for g in (g_pre_mix, g_post_mix, g_pre_ffn, g_post_ffn))
    pscale = pool_scale[0].reshape(1, dm)
    bf_col = b_forget[0].reshape(nh, 1)
    bf_pad = jnp.pad(b_forget[0].reshape(1, nh), ((0, 0), (0, LANES - nh)))

    meta = jnp.broadcast_to(meta_tokens[None], (nb, n_meta, dm))
    xp = jnp.concatenate([jnp.zeros((nb, pad, dm), F32), meta, x_prompt], axis=1).reshape(mp, dm)
    h1 = _rmsnorm(xp, g1, BF16)
    zqkv = _matmul(h1, w_qkv)
    z_rest = _matmul(h1, w_rest)
    lf_t, c_t = _forget_prompt(h1, wf_t, bf_col, lp, pad)
    a = _attn_prompt(zqkv, c_t, who, nb, lp, nh, dh)
    d = _pool_prompt(z_rest, pw, lp, pad)
    mix = _poolmix(d, wpm, z_rest, a, pscale, pw)
    mm = _matmul(mix, wo)
    x2, h2 = _post_mix(mm, xp, g2, g3)
    gact, tails = _ffn_gu_prompt(h2, wgu, cw, cb, lp, pad)
    f = _matmul(gact, wd, tm_cap=704, x_buffers=1, name="ffn_down")
    y = _post_ffn(f, x2, g4)

    first = pad + n_meta
    y_prompt = y.reshape(nb, lp, dm)[:, first:]
    k_prompt = zqkv[:, aw:2 * aw].reshape(nb, lp, nh, dh)[:, pad:][None]
    v_prompt = zqkv[:, 2 * aw:].reshape(nb, lp, nh, dh)[:, pad:][None]
    logf_prompt = lf_t.T.reshape(nb, lp, nh)[:, pad:][None]
    pool_prompt = z_rest[:, :pw].reshape(nb, lp, pw)[:, lp - n_state:][None]
    tiles_per_seq = tails.shape[0] // nb
    conv_prompt = tails.reshape(nb, tiles_per_seq, CONV_TAIL, dff)[:, -1, CONV_TAIL - 2:][None]

    xs = x_sample.reshape(db, dm)
    h1s = _rmsnorm(xs, g1, BF16)
    zqkv_s = _matmul(h1s, w_qkv)
    z_rest_s = _matmul(h1s, w_rest)
    fl_s = _matmul(h1s, wf_pad)
    o_s, lf_s = _attn_sample(page_table, zqkv_s, fl_s, bf_pad, cache_k, cache_v, cache_logf[0], nh, dh)
    a_s = _head_out(o_s, who)
    d_s = _pool_sample(state_pool[0].transpose(1, 0, 2), z_rest_s, pw, past_len)
    mix_s = _poolmix(d_s, wpm, z_rest_s, a_s, pscale, pw)
    mm_s = _matmul(mix_s, wo)
    x2s, h2s = _post_mix(mm_s, xs, g2, g3)
    gact_s, a_new = _ffn_gu_sample(h2s, wgu, cw, cb, state_conv[0][:, 0], state_conv[0][:, 1])
    f_s = _matmul(gact_s, wd, name="ffn_down")
    y_s = _post_ffn(f_s, x2s, g4)

    y_sample = y_s.reshape(db, 1, dm)
    k_sample = zqkv_s[:, aw:2 * aw].reshape(1, db, 1, nh, dh)
    v_sample = zqkv_s[:, 2 * aw:].reshape(1, db, 1, nh, dh)
    logf_sample = lf_s.reshape(1, db, 1, nh)
    u_s = z_rest_s[:, :pw]
    pool_sample = jnp.concatenate([state_pool[0][:, 1:], u_s[:, None]], axis=1)[None]
    conv_sample = jnp.stack([state_conv[0][:, 1], a_new], axis=1)[None]

    return (y_prompt, y_sample, k_prompt, v_prompt, logf_prompt, pool_prompt, conv_prompt,
            k_sample, v_sample, logf_sample, pool_sample, conv_sample)
```

```python
import functools
import math

import jax
import jax.numpy as jnp
from jax import lax
from jax.experimental import pallas as pl
from jax.experimental.pallas import tpu as pltpu

F32 = jnp.float32
BF16 = jnp.bfloat16
RMS_EPS = 1e-6
POOL_WINDOWS = (2, 4, 8, 16)
LANES = 128
BF16_SUBLANES = 16
HALO = 16
CONV_TAIL = 8
MASKED_CUMLOG = 1e30
LOG2E = math.log2(math.e)
HEADS_PER_STEP = 2
VMEM_LIMIT = 56 * 1024 * 1024


def _pick_tile(n, cap, mult):
    best = None
    for t in range(mult, min(n, cap) + 1, mult):
        if n % t == 0:
            best = t
    return n if best is None else best


def _params(*sem):
    return pltpu.CompilerParams(dimension_semantics=sem, vmem_limit_bytes=VMEM_LIMIT)


def _rms(x, g):
    return x * lax.rsqrt(jnp.mean(x * x, axis=-1, keepdims=True) + RMS_EPS) * g


def _sigmoid(x):
    return 1.0 / (1.0 + jnp.exp(-x))


def _log_sigmoid(x):
    return jnp.minimum(x, 0.0) - jnp.log(1.0 + jnp.exp(-jnp.abs(x)))


def _gelu_tanh(x):
    cdf = 0.5 * (1.0 + jnp.tanh(math.sqrt(2.0 / math.pi) * (x + 0.044715 * (x * x * x))))
    return x * cdf


def _split3(x):
    hi = x.astype(BF16)
    r = x - hi.astype(F32)
    mid = r.astype(BF16)
    lo = (r - mid.astype(F32)).astype(BF16)
    return hi, mid, lo


_NN = (((1,), (0,)), ((), ()))
_NT = (((1,), (1,)), ((), ()))


def _dot_f32_lhs(x, m_bf, dims):
    out = None
    for piece in _split3(x):
        t = lax.dot_general(piece, m_bf, dims, preferred_element_type=F32)
        out = t if out is None else out + t
    return out


def _dot_f32_rhs(m_bf, x, dims):
    out = None
    for piece in _split3(x):
        t = lax.dot_general(m_bf, piece, dims, preferred_element_type=F32)
        out = t if out is None else out + t
    return out


def _rmsnorm_kernel(x_ref, g_ref, o_ref):
    o_ref[...] = _rms(x_ref[...], g_ref[...]).astype(o_ref.dtype)


def _rmsnorm(x, g, out_dtype):
    m, d = x.shape
    tr = _pick_tile(m, 256, BF16_SUBLANES)
    return pl.pallas_call(
        _rmsnorm_kernel,
        grid=(m // tr,),
        in_specs=[pl.BlockSpec((tr, d), lambda i: (i, 0)), pl.BlockSpec((1, d), lambda i: (0, 0))],
        out_specs=pl.BlockSpec((tr, d), lambda i: (i, 0)),
        out_shape=jax.ShapeDtypeStruct((m, d), out_dtype),
        compiler_params=_params("parallel"),
        name="rmsnorm",
    )(x, g)


def _post_mix_kernel(m_ref, x_ref, g1_ref, g2_ref, x2_ref, h2_ref):
    x2 = x_ref[...] + _rms(m_ref[...], g1_ref[...])
    x2_ref[...] = x2
    h2_ref[...] = _rms(x2, g2_ref[...]).astype(h2_ref.dtype)


def _post_mix(mm, x, g1, g2):
    m, d = x.shape
    tr = _pick_tile(m, 256, BF16_SUBLANES)
    row = pl.BlockSpec((tr, d), lambda i: (i, 0))
    vec = pl.BlockSpec((1, d), lambda i: (0, 0))
    return pl.pallas_call(
        _post_mix_kernel,
        grid=(m // tr,),
        in_specs=[row, row, vec, vec],
        out_specs=[row, row],
        out_shape=[jax.ShapeDtypeStruct((m, d), F32), jax.ShapeDtypeStruct((m, d), BF16)],
        compiler_params=_params("parallel"),
        name="post_mix",
    )(mm, x, g1, g2)


def _post_ffn_kernel(f_ref, x_ref, g_ref, y_ref):
    y_ref[...] = (x_ref[...] + _rms(f_ref[...], g_ref[...])).reshape(y_ref.shape)


def _post_ffn(f, x, g):
    m, d = x.shape
    tr = _pick_tile(m, 256, 8)
    row = pl.BlockSpec((tr, d), lambda i: (i, 0))
    vec = pl.BlockSpec((1, d), lambda i: (0, 0))
    return pl.pallas_call(
        _post_ffn_kernel,
        grid=(m // tr,),
        in_specs=[row, row, vec],
        out_specs=row,
        out_shape=jax.ShapeDtypeStruct((m, d), F32),
        compiler_params=_params("parallel"),
        name="post_ffn",
    )(f, x, g)


def _prompt_x(head_ref, x_ref):
    return jnp.where(pl.program_id(1) == 0, head_ref[...], x_ref[0])


def _rmsnorm_prompt_kernel(head_ref, x_ref, g_ref, o_ref):
    o_ref[...] = _rms(_prompt_x(head_ref, x_ref), g_ref[...]).astype(o_ref.dtype)


def _post_mix_prompt_kernel(m_ref, head_ref, x_ref, g1_ref, g2_ref, x2_ref, h2_ref):
    x2 = _prompt_x(head_ref, x_ref) + _rms(m_ref[...], g1_ref[...])
    x2_ref[...] = x2
    h2_ref[...] = _rms(x2, g2_ref[...]).astype(h2_ref.dtype)


def _prompt_specs(tr, d, tiles):
    head = pl.BlockSpec((tr, d), lambda b, t: (0, 0))
    x = pl.BlockSpec((1, tr, d), lambda b, t: (b, jnp.maximum(t - 1, 0), 0))
    row = pl.BlockSpec((tr, d), lambda b, t: (b * tiles + t, 0))
    vec = pl.BlockSpec((1, d), lambda b, t: (0, 0))
    return head, x, row, vec


def _rmsnorm_prompt(head, x_prompt, g, lp):
    nb, _, d = x_prompt.shape
    tr = head.shape[0]
    tiles = lp // tr
    head_s, x_s, row, vec = _prompt_specs(tr, d, tiles)
    return pl.pallas_call(
        _rmsnorm_prompt_kernel,
        grid=(nb, tiles),
        in_specs=[head_s, x_s, vec],
        out_specs=row,
        out_shape=jax.ShapeDtypeStruct((nb * lp, d), BF16),
        compiler_params=_params("parallel", "arbitrary"),
        name="rmsnorm_prompt",
    )(head, x_prompt, g)


def _post_mix_prompt(mm, head, x_prompt, g1, g2, lp):
    nb, _, d = x_prompt.shape
    tr = head.shape[0]
    tiles = lp // tr
    head_s, x_s, row, vec = _prompt_specs(tr, d, tiles)
    return pl.pallas_call(
        _post_mix_prompt_kernel,
        grid=(nb, tiles),
        in_specs=[row, head_s, x_s, vec, vec],
        out_specs=[row, row],
        out_shape=[jax.ShapeDtypeStruct((nb * lp, d), F32), jax.ShapeDtypeStruct((nb * lp, d), BF16)],
        compiler_params=_params("parallel", "arbitrary"),
        name="post_mix_prompt",
    )(mm, head, x_prompt, g1, g2)


def _post_ffn_prompt(f, x2, g, nb, seq, lp, tr):
    d = x2.shape[1]
    tiles = lp // tr
    skip = tiles - seq // tr
    row = pl.BlockSpec((tr, d), lambda b, t: (b * tiles + skip + t, 0))
    return pl.pallas_call(
        _post_ffn_kernel,
        grid=(nb, seq // tr),
        in_specs=[row, row, pl.BlockSpec((1, d), lambda b, t: (0, 0))],
        out_specs=pl.BlockSpec((1, tr, d), lambda b, t: (b, t, 0)),
        out_shape=jax.ShapeDtypeStruct((nb, seq, d), F32),
        compiler_params=_params("parallel", "arbitrary"),
        name="post_ffn_prompt",
    )(f, x2, g)


def _mm_kernel(x_ref, w_ref, o_ref):
    o_ref[...] = jnp.dot(x_ref[...], w_ref[...], preferred_element_type=F32).astype(o_ref.dtype)


def _matmul(x, w, out_dtype=F32, tm_cap=1408, tn_cap=512, x_buffers=2, n_cols=None, name="matmul"):
    m, k = x.shape
    n = w.shape[1] if n_cols is None else n_cols
    tm = _pick_tile(m, tm_cap, BF16_SUBLANES)
    tn = _pick_tile(n, tn_cap, LANES)
    x_mode = {} if x_buffers == 2 else {"pipeline_mode": pl.Buffered(x_buffers)}
    x_spec = pl.BlockSpec((tm, k), lambda i, j: (i, 0), **x_mode)
    return pl.pallas_call(
        _mm_kernel,
        grid=(m // tm, n // tn),
        in_specs=[x_spec, pl.BlockSpec((k, tn), lambda i, j: (0, j))],
        out_specs=pl.BlockSpec((tm, tn), lambda i, j: (i, j)),
        out_shape=jax.ShapeDtypeStruct((m, n), out_dtype),
        compiler_params=_params("parallel", "arbitrary"),
        name=name,
    )(x, w)


def _forget_kernel(h_ref, wf_ref, b_ref, lf_ref, c_ref, carry_ref, *, tc, lp, pad):
    i = pl.program_id(0)
    off = lax.rem(i * tc, lp)
    fl = lax.dot_general(wf_ref[...], h_ref[...], _NT, preferred_element_type=F32)
    lf = _log_sigmoid(fl + b_ref[...])
    pos = off + lax.broadcasted_iota(jnp.int32, (1, tc), 1)
    valid = pos >= pad
    lfm = jnp.where(valid, lf, 0.0)
    upper = (lax.broadcasted_iota(jnp.int32, (tc, tc), 0)
             <= lax.broadcasted_iota(jnp.int32, (tc, tc), 1)).astype(BF16)
    cs = _dot_f32_lhs(lfm, upper, _NN)

    @pl.when(off == 0)
    def _():
        carry_ref[...] = jnp.zeros_like(carry_ref)

    c = cs + carry_ref[:, 0:1]
    carry_ref[...] = jnp.broadcast_to(c[:, tc - 1:tc], carry_ref.shape)
    lf_ref[...] = lf
    c_ref[...] = jnp.where(valid, c * LOG2E, MASKED_CUMLOG)


def _forget_prompt(h, wf_t, b_col, lp, pad):
    mp, d = h.shape
    nh = wf_t.shape[0]
    tc = _pick_tile(lp, 384, LANES)
    kern = functools.partial(_forget_kernel, tc=tc, lp=lp, pad=pad)
    return pl.pallas_call(
        kern,
        grid=(mp // tc,),
        in_specs=[pl.BlockSpec((tc, d), lambda i: (i, 0)),
                  pl.BlockSpec((nh, d), lambda i: (0, 0)),
                  pl.BlockSpec((nh, 1), lambda i: (0, 0))],
        out_specs=[pl.BlockSpec((nh, tc), lambda i: (0, i)), pl.BlockSpec((nh, tc), lambda i: (0, i))],
        out_shape=[jax.ShapeDtypeStruct((nh, mp), F32), jax.ShapeDtypeStruct((nh, mp), F32)],
        scratch_shapes=[pltpu.VMEM((nh, LANES), F32)],
        compiler_params=_params("arbitrary"),
        name="forget_prompt",
    )(h, wf_t, b_col)


def _attn_kernel(q_ref, k_ref, v_ref, c_ref, who_ref, a_ref, kb_ref, vb_ref, *, t, dh, scale):
    iq = pl.program_id(2)
    hps = who_ref.shape[0]
    ho = who_ref.shape[2]

    @pl.when(iq == 0)
    def _():
        kb_ref[...] = k_ref[...].astype(BF16)
        vb_ref[...] = v_ref[...].astype(BF16)

    qs = [q_ref[:, hh * dh:(hh + 1) * dh].astype(BF16) for hh in range(hps)]
    rows = iq * t + lax.broadcasted_iota(jnp.int32, (t, 1), 0)

    def step(j, carry, diagonal):
        start = pl.multiple_of(j * t, t)
        out = []
        for hh in range(hps):
            m, l, acc = carry[hh]
            ks = kb_ref[pl.ds(start, t), hh * dh:(hh + 1) * dh]
            vs = vb_ref[pl.ds(start, t), hh * dh:(hh + 1) * dh]
            s = lax.dot_general(qs[hh], ks, _NT, preferred_element_type=F32)
            tt = s * (scale * LOG2E) - c_ref[hh, j]
            if diagonal:
                cols = j * t + lax.broadcasted_iota(jnp.int32, (1, t), 1)
                tt = jnp.where(cols <= rows, tt, -jnp.inf)
            m_new = jnp.maximum(m, jnp.max(tt, axis=-1, keepdims=True))
            alpha = jnp.exp2(m - m_new)
            p = jnp.exp2(tt - m_new)
            l = alpha * l + jnp.sum(p, axis=-1, keepdims=True)
            acc = alpha * acc + jnp.dot(p.astype(BF16), vs, preferred_element_type=F32)
            out.append((m_new, l, acc))
        return tuple(out)

    one = (jnp.full((t, 1), -jnp.inf, F32), jnp.zeros((t, 1), F32), jnp.zeros((t, dh), F32))
    carry = lax.fori_loop(0, iq, functools.partial(step, diagonal=False), (one,) * hps)
    carry = step(iq, carry, True)
    for hh in range(hps):
        _, l, acc = carry[hh]
        o = (acc / l).astype(BF16)
        a_ref[:, hh * ho:(hh + 1) * ho] = jnp.dot(o, who_ref[hh], preferred_element_type=F32)


def _attn_prompt(zqkv, c_t, who, nb, lp, nh, dh):
    mp = zqkv.shape[0]
    ho = who.shape[2]
    t = _pick_tile(lp, 384, LANES)
    nq = lp // t
    hps = HEADS_PER_STEP
    ng = nh // hps
    c4 = c_t.reshape(nh, mp // t, 1, t)
    kern = functools.partial(_attn_kernel, t=t, dh=dh, scale=dh ** -0.5)
    return pl.pallas_call(
        kern,
        grid=(nb, ng, nq),
        in_specs=[pl.BlockSpec((t, hps * dh), lambda b, h, i: (b * nq + i, h)),
                  pl.BlockSpec((lp, hps * dh), lambda b, h, i: (b, ng + h)),
                  pl.BlockSpec((lp, hps * dh), lambda b, h, i: (b, 2 * ng + h)),
                  pl.BlockSpec((hps, nq, 1, t), lambda b, h, i: (h, b, 0, 0)),
                  pl.BlockSpec((hps, dh, ho), lambda b, h, i: (h, 0, 0))],
        out_specs=pl.BlockSpec((t, hps * ho), lambda b, h, i: (b * nq + i, h)),
        out_shape=jax.ShapeDtypeStruct((mp, nh * ho), F32),
        scratch_shapes=[pltpu.VMEM((lp, hps * dh), BF16), pltpu.VMEM((lp, hps * dh), BF16)],
        compiler_params=_params("parallel", "parallel", "arbitrary"),
        name="attn_prompt",
    )(zqkv, zqkv, zqkv, c4, who)


def _softmax_update(tiles, m_ref, l_ref, acc_ref, pv):
    mt = tiles[0]
    for t in tiles[1:]:
        mt = jnp.maximum(mt, t)
    m_old = m_ref[:, 0:1]
    m_new = jnp.maximum(m_old, jnp.max(mt, axis=1, keepdims=True))
    alpha = jnp.exp(m_old - m_new)
    ps = [jnp.exp(t - m_new) for t in tiles]
    tot = ps[0]
    for p in ps[1:]:
        tot = tot + p
    l_ref[...] = jnp.broadcast_to(alpha * l_ref[:, 0:1] + jnp.sum(tot, axis=1, keepdims=True), l_ref.shape)
    acc_ref[...] = alpha * acc_ref[...] + pv(ps)
    m_ref[...] = jnp.broadcast_to(m_new, m_ref.shape)


def _sattn_kernel(pt_ref, q_ref, kn_ref, vn_ref, fl_ref, bf_ref, w2_ref, *refs, nh, dh, ps, gp, scale):
    del pt_ref
    k_refs, v_refs, lf_refs = refs[0:gp], refs[gp:2 * gp], refs[2 * gp:3 * gp]
    o_ref, lfo_ref = refs[3 * gp], refs[3 * gp + 1]
    m_ref, l_ref, carry_ref, acc_ref, lfs_ref = refs[3 * gp + 2:]
    step = pl.program_id(1)
    rows = ps * nh
    nchunk = rows // LANES
    lane = lax.broadcasted_iota(jnp.int32, (nh, LANES), 1)
    diag = (lane & (nh - 1)) == lax.broadcasted_iota(jnp.int32, (nh, LANES), 0)

    @pl.when(step == 0)
    def _():
        m_ref[...] = jnp.full_like(m_ref, -jnp.inf)
        l_ref[...] = jnp.zeros_like(l_ref)
        carry_ref[...] = jnp.zeros_like(carry_ref)
        acc_ref[...] = jnp.zeros_like(acc_ref)

    qb = q_ref[0].astype(BF16)

    for g in range(gp):
        lfs_ref[g:g + 1, :] = lf_refs[g][0]
    x = lfs_ref[...]
    w2 = w2_ref[...]
    run = jnp.zeros((gp, LANES), F32)
    cks = []
    for k in range(nchunk):
        r = _dot_f32_lhs(x[:, k * LANES:(k + 1) * LANES], w2, _NN)
        cks.append(r[:, :LANES] + run)
        run = run + r[:, LANES:]
    earlier = (lax.broadcasted_iota(jnp.int32, (gp, gp), 0)
               > lax.broadcasted_iota(jnp.int32, (gp, gp), 1)).astype(BF16)
    offs = _dot_f32_rhs(earlier, run, _NN) + carry_ref[0:1, :]
    carry_new = offs[gp - 1:gp] + run[gp - 1:gp]

    tiles = []
    for g in range(gp):
        kf = k_refs[g][0, 0].reshape(rows, dh).astype(BF16)
        zt = lax.dot_general(qb, kf, _NT, preferred_element_type=F32)
        for k in range(nchunk):
            c_row = cks[k][g:g + 1] + offs[g:g + 1]
            tiles.append(jnp.where(diag, zt[:, k * LANES:(k + 1) * LANES] * scale - c_row, -jnp.inf))

    def pv(probs):
        out = None
        for g in range(gp):
            pm = jnp.concatenate(probs[g * nchunk:(g + 1) * nchunk], axis=1).astype(BF16)
            vf = v_refs[g][0, 0].reshape(rows, dh).astype(BF16)
            o = jnp.dot(pm, vf, preferred_element_type=F32)
            out = o if out is None else out + o
        return out

    _softmax_update(tiles, m_ref, l_ref, acc_ref, pv)
    carry_ref[...] = jnp.broadcast_to(carry_new, carry_ref.shape)

    @pl.when(step == pl.num_programs(1) - 1)
    def _():
        lf_new = _log_sigmoid(fl_ref[0] + bf_ref[...])
        lfo_ref[0] = lf_new
        c_new = carry_new + lf_new
        zeros = jnp.zeros((LANES - nh, dh), F32)
        kpad = jnp.concatenate([kn_ref[0], zeros], axis=0).astype(BF16)
        vpad = jnp.concatenate([vn_ref[0], zeros], axis=0).astype(BF16)
        zt = lax.dot_general(qb, kpad, _NT, preferred_element_type=F32)
        tile = jnp.where(diag & (lane < nh), zt * scale - c_new, -jnp.inf)
        _softmax_update([tile], m_ref, l_ref, acc_ref,
                        lambda probs: jnp.dot(probs[0].astype(BF16), vpad, preferred_element_type=F32))
        o_ref[0] = acc_ref[...] / l_ref[:, 0:1]


def _attn_sample(page_table, zqkv_s, fl_s, bf_pad, cache_k, cache_v, cache_lf, nh, dh):
    db, n_pages = page_table.shape
    n_phys, ps = cache_k.shape[1], cache_k.shape[2]
    assert nh & (nh - 1) == 0 and LANES % nh == 0 and nh % 8 == 0
    rows = ps * nh
    gp = _pick_tile(n_pages, 8, 1)
    qkv_h = zqkv_s.reshape(db, 3 * nh, dh)
    fl3 = fl_s.reshape(db, 1, LANES)
    lf_flat = cache_lf.reshape(n_phys, 1, rows)
    r = jnp.arange(LANES)
    same = (r[:, None] % nh) == (r[None, :] % nh)
    w2 = jnp.concatenate([same & (r[:, None] <= r[None, :]), same], axis=1).astype(BF16)
    kern = functools.partial(_sattn_kernel, nh=nh, dh=dh, ps=ps, gp=gp, scale=dh ** -0.5)

    def page_spec(g):
        return pl.BlockSpec((1, 1, ps, nh, dh), lambda b, s, pt: (0, pt[b, s * gp + g], 0, 0, 0))

    def lf_spec(g):
        return pl.BlockSpec((1, 1, rows), lambda b, s, pt: (pt[b, s * gp + g], 0, 0))

    grid_spec = pltpu.PrefetchScalarGridSpec(
        num_scalar_prefetch=1,
        grid=(db, n_pages // gp),
        in_specs=[pl.BlockSpec((1, nh, dh), lambda b, s, pt: (b, 0, 0)),
                  pl.BlockSpec((1, nh, dh), lambda b, s, pt: (b, 1, 0)),
                  pl.BlockSpec((1, nh, dh), lambda b, s, pt: (b, 2, 0)),
                  pl.BlockSpec((1, 1, LANES), lambda b, s, pt: (b, 0, 0)),
                  pl.BlockSpec((1, LANES), lambda b, s, pt: (0, 0)),
                  pl.BlockSpec((LANES, 2 * LANES), lambda b, s, pt: (0, 0))]
                 + [page_spec(g) for g in range(gp)] + [page_spec(g) for g in range(gp)]
                 + [lf_spec(g) for g in range(gp)],
        out_specs=[pl.BlockSpec((1, nh, dh), lambda b, s, pt: (b, 0, 0)),
                   pl.BlockSpec((1, 1, LANES), lambda b, s, pt: (b, 0, 0))],
        scratch_shapes=[pltpu.VMEM((nh, LANES), F32), pltpu.VMEM((nh, LANES), F32),
                        pltpu.VMEM((8, LANES), F32), pltpu.VMEM((nh, dh), F32),
                        pltpu.VMEM((gp, rows), F32)],
    )
    o3, lf3 = pl.pallas_call(
        kern,
        grid_spec=grid_spec,
        out_shape=[jax.ShapeDtypeStruct((db, nh, dh), F32), jax.ShapeDtypeStruct((db, 1, LANES), F32)],
        compiler_params=_params("parallel", "arbitrary"),
        name="attn_sample",
    )(page_table, qkv_h, qkv_h, qkv_h, fl3, bf_pad, w2,
      *([cache_k] * gp), *([cache_v] * gp), *([lf_flat] * gp))
    return o3.reshape(db, nh * dh), lf3.reshape(db, LANES)[:, :nh]


def _head_out_kernel(o_ref, w_ref, a_ref):
    a_ref[...] = jnp.dot(o_ref[...].astype(BF16), w_ref[0], preferred_element_type=F32)


def _head_out(o, who):
    m = o.shape[0]
    nh, dh, ho = who.shape
    return pl.pallas_call(
        _head_out_kernel,
        grid=(nh,),
        in_specs=[pl.BlockSpec((m, dh), lambda h: (0, h)), pl.BlockSpec((1, dh, ho), lambda h: (h, 0, 0))],
        out_specs=pl.BlockSpec((m, ho), lambda h: (0, h)),
        out_shape=jax.ShapeDtypeStruct((m, nh * ho), F32),
        compiler_params=_params("parallel"),
        name="head_out",
    )(o, who)


def _pool_prompt_kernel(halo_ref, u_ref, d_ref, ext_ref, *, tr, lp, pad, pg):
    i = pl.program_id(0)
    ext_ref[0:HALO, :] = halo_ref[...]
    ext_ref[HALO:, :] = u_ref[...]
    pos = lax.rem(i * tr, lp) - pad + lax.broadcasted_iota(jnp.int32, (tr, 1), 0)
    for g, w in enumerate(POOL_WINDOWS):
        cols = slice(g * pg, (g + 1) * pg)
        u = ext_ref[pl.ds(HALO, tr), cols]
        acc = u
        for k in range(1, w):
            acc = acc + ext_ref[pl.ds(HALO - k, tr), cols]
        cnt = jnp.clip(pos + 1, 1, w).astype(F32)
        d_ref[:, cols] = (acc / cnt - u).astype(d_ref.dtype)


def _pool_prompt(z_rest, pw, lp, pad):
    mp = z_rest.shape[0]
    tr = _pick_tile(lp, 384, BF16_SUBLANES)
    kern = functools.partial(_pool_prompt_kernel, tr=tr, lp=lp, pad=pad, pg=pw // len(POOL_WINDOWS))
    return pl.pallas_call(
        kern,
        grid=(mp // tr,),
        in_specs=[pl.BlockSpec((HALO, pw), lambda i: (jnp.maximum(i * (tr // HALO) - 1, 0), 0)),
                  pl.BlockSpec((tr, pw), lambda i: (i, 0))],
        out_specs=pl.BlockSpec((tr, pw), lambda i: (i, 0)),
        out_shape=jax.ShapeDtypeStruct((mp, pw), BF16),
        scratch_shapes=[pltpu.VMEM((tr + HALO, pw), F32)],
        compiler_params=_params("parallel"),
        name="pool_prompt",
    )(z_rest, z_rest)


def _pool_sample_kernel(st_ref, u_ref, d_ref, *, cnts, pg):
    u_all = u_ref[...]
    n_state = st_ref.shape[0]
    for g, cnt in enumerate(cnts):
        cols = slice(g * pg, (g + 1) * pg)
        u = u_all[:, cols]
        acc = u
        for k in range(1, cnt):
            acc = acc + st_ref[n_state - k, :, cols]
        d_ref[:, cols] = (acc / float(cnt) - u).astype(d_ref.dtype)


def _pool_sample(state_t, z_rest_s, pw, past_len):
    n_state, db, _ = state_t.shape
    tb = _pick_tile(db, 16, BF16_SUBLANES)
    cnts = tuple(min(past_len + 1, w) for w in POOL_WINDOWS)
    kern = functools.partial(_pool_sample_kernel, cnts=cnts, pg=pw // len(POOL_WINDOWS))
    return pl.pallas_call(
        kern,
        grid=(db // tb,),
        in_specs=[pl.BlockSpec((n_state, tb, pw), lambda i: (0, i, 0)),
                  pl.BlockSpec((tb, pw), lambda i: (i, 0))],
        out_specs=pl.BlockSpec((tb, pw), lambda i: (i, 0)),
        out_shape=jax.ShapeDtypeStruct((db, pw), BF16),
        compiler_params=_params("parallel"),
        name="pool_sample",
    )(state_t, z_rest_s)


def _poolmix_kernel(d_ref, w_ref, ga_ref, gp_ref, a_ref, ps_ref, o_ref):
    p = jnp.dot(d_ref[...], w_ref[0], preferred_element_type=F32) * ps_ref[...]
    o_ref[...] = (_sigmoid(ga_ref[...]) * a_ref[...] + _sigmoid(gp_ref[...]) * p).astype(o_ref.dtype)


def _poolmix(d, wpm, z_rest, a, pool_scale, pw):
    m, dm = a.shape
    ng, pg, pgo = wpm.shape
    tm = _pick_tile(m, 1056, BF16_SUBLANES)
    tn = pgo
    assert pw % tn == 0 and dm % tn == 0
    ga0 = pw // tn
    gp0 = (pw + dm) // tn
    return pl.pallas_call(
        _poolmix_kernel,
        grid=(m // tm, ng),
        in_specs=[pl.BlockSpec((tm, pg), lambda i, g: (i, g)),
                  pl.BlockSpec((1, pg, tn), lambda i, g: (g, 0, 0)),
                  pl.BlockSpec((tm, tn), lambda i, g: (i, ga0 + g)),
                  pl.BlockSpec((tm, tn), lambda i, g: (i, gp0 + g)),
                  pl.BlockSpec((tm, tn), lambda i, g: (i, g)),
                  pl.BlockSpec((1, tn), lambda i, g: (0, g))],
        out_specs=pl.BlockSpec((tm, tn), lambda i, g: (i, g)),
        out_shape=jax.ShapeDtypeStruct((m, dm), BF16),
        compiler_params=_params("parallel", "arbitrary"),
        name="poolmix",
    )(d, wpm, z_rest, z_rest, a, pool_scale)


def _conv_gate(a, a1, a2, b, cw_ref, cb_ref):
    cw = cw_ref[...]
    conv = cb_ref[...] + cw[0:1] * a2 + cw[1:2] * a1 + cw[2:3] * a
    return (_gelu_tanh(conv) * b).astype(BF16)


def _ffn_gu_prompt_kernel(h_ref, wg_ref, wu_ref, cw_ref, cb_ref, g_ref, tail_ref, aext_ref, tails_ref,
                          *, tm, tf, lp, pad):
    i = pl.program_id(0)
    j = pl.program_id(1)
    h = h_ref[...]
    a = jnp.dot(h, wg_ref[...], preferred_element_type=F32)
    b = jnp.dot(h, wu_ref[...], preferred_element_type=F32)

    @pl.when(i == 0)
    def _():
        tails_ref[j] = jnp.zeros((CONV_TAIL, tf), F32)

    aext_ref[0:CONV_TAIL, :] = tails_ref[j]
    aext_ref[CONV_TAIL:, :] = a
    pos = lax.rem(i * tm, lp) - pad + lax.broadcasted_iota(jnp.int32, (tm, 1), 0)
    a1 = jnp.where(pos >= 1, aext_ref[pl.ds(CONV_TAIL - 1, tm), :], 0.0)
    a2 = jnp.where(pos >= 2, aext_ref[pl.ds(CONV_TAIL - 2, tm), :], 0.0)
    last = a[tm - CONV_TAIL:, :]
    tails_ref[j] = last
    tail_ref[0] = last
    g_ref[...] = _conv_gate(a, a1, a2, b, cw_ref, cb_ref)


def _ffn_gu_sample_kernel(h_ref, wg_ref, wu_ref, cw_ref, cb_ref, s0_ref, s1_ref, g_ref, a_ref):
    h = h_ref[...]
    a = jnp.dot(h, wg_ref[...], preferred_element_type=F32)
    b = jnp.dot(h, wu_ref[...], preferred_element_type=F32)
    a_ref[...] = a
    g_ref[...] = _conv_gate(a, s1_ref[...], s0_ref[...], b, cw_ref, cb_ref)


def _ffn_gu_specs(tm, dm, tf):
    return [pl.BlockSpec((tm, dm), lambda i, j: (i, 0)),
            pl.BlockSpec((dm, tf), lambda i, j: (0, j)),
            pl.BlockSpec((dm, tf), lambda i, j: (0, j)),
            pl.BlockSpec((3, tf), lambda i, j: (0, j)),
            pl.BlockSpec((1, tf), lambda i, j: (0, j))]


def _ffn_gu_prompt(h2, wg, wu, cw, cb, tf, lp, pad):
    mp, dm = h2.shape
    nj = wg.shape[1] // tf
    tm = _pick_tile(lp, 1408, BF16_SUBLANES)
    kern = functools.partial(_ffn_gu_prompt_kernel, tm=tm, tf=tf, lp=lp, pad=pad)
    return pl.pallas_call(
        kern,
        grid=(mp // tm, nj),
        in_specs=_ffn_gu_specs(tm, dm, tf),
        out_specs=[pl.BlockSpec((tm, tf), lambda i, j: (i, j)),
                   pl.BlockSpec((1, CONV_TAIL, tf), lambda i, j: (i, 0, j))],
        out_shape=[jax.ShapeDtypeStruct((mp, nj * tf), BF16),
                   jax.ShapeDtypeStruct((mp // tm, CONV_TAIL, nj * tf), F32)],
        scratch_shapes=[pltpu.VMEM((tm + CONV_TAIL, tf), F32), pltpu.VMEM((nj, CONV_TAIL, tf), F32)],
        compiler_params=_params("arbitrary", "arbitrary"),
        name="ffn_gu_prompt",
    )(h2, wg, wu, cw, cb)


def _ffn_gu_sample(h2, wg, wu, cw, cb, tf, s0, s1):
    m, dm = h2.shape
    nj = wg.shape[1] // tf
    kern = _ffn_gu_sample_kernel
    col = pl.BlockSpec((m, tf), lambda i, j: (0, j))
    return pl.pallas_call(
        kern,
        grid=(1, nj),
        in_specs=_ffn_gu_specs(m, dm, tf) + [col, col],
        out_specs=[col, col],
        out_shape=[jax.ShapeDtypeStruct((m, nj * tf), BF16), jax.ShapeDtypeStruct((m, nj * tf), F32)],
        compiler_params=_params("arbitrary", "arbitrary"),
        name="ffn_gu_sample",
    )(h2, wg, wu, cw, cb, s0, s1)


def kernel(x_prompt, x_sample, cache_k, cache_v, cache_logf, state_pool, state_conv, page_table, meta_tokens, g_pre_mix, g_post_mix, g_pre_ffn, g_post_ffn, w_in, b_forget, w_head_out, w_pool_map, pool_scale, w_out, w_gate, w_up, conv_w, conv_b, w_down):
    nb, seq, dm = x_prompt.shape
    db = x_sample.shape[0]
    depth, _, ps, nh, dh = cache_k.shape
    assert depth == 1 and x_sample.shape[1] == 1
    n_meta = meta_tokens.shape[0]
    n_state, pw = state_pool.shape[2], state_pool.shape[3]
    dff = w_gate.shape[2]
    aw = nh * dh
    past_len = page_table.shape[1] * ps
    assert n_state <= HALO - 1 and w_in.shape[2] == 3 * aw + nh + pw + 2 * dm

    pad = (-(n_meta + seq)) % LANES
    if pad < HALO:
        pad += LANES
    lp = pad + n_meta + seq
    mp = nb * lp

    w_in_bf = w_in[0].astype(BF16)
    w_f = w_in[0, :, 3 * aw:3 * aw + nh]
    w_rest = w_in_bf[:, 3 * aw + nh:]
    wf_t = w_f.T.astype(BF16)
    wf_pad = jnp.pad(w_f, ((0, 0), (0, LANES - nh))).astype(BF16)
    who = w_head_out[0].astype(BF16)
    wpm = w_pool_map[0].astype(BF16)
    wo = w_out[0].astype(BF16)
    tf = _pick_tile(dff, 256, LANES)
    wg = w_gate[0].astype(BF16)
    wu = w_up[0].astype(BF16)
    wd = w_down[0].astype(BF16)
    cw = conv_w[0]
    cb = conv_b[0].reshape(1, dff)
    g1, g2, g3, g4 = (g[0].reshape(1, dm) for g in (g_pre_mix, g_post_mix, g_pre_ffn, g_post_ffn))
    pscale = pool_scale[0].reshape(1, dm)
    bf_col = b_forget[0].reshape(nh, 1)
    bf_pad = jnp.pad(b_forget[0].reshape(1, nh), ((0, 0), (0, LANES - nh)))

    head = jnp.concatenate([jnp.zeros((pad, dm), F32), meta_tokens], axis=0)
    assert seq % head.shape[0] == 0
    h1 = _rmsnorm_prompt(head, x_prompt, g1, lp)
    zqkv = _matmul(h1, w_in_bf, n_cols=3 * aw)
    z_rest = _matmul(h1, w_rest)
    lf_t, c_t = _forget_prompt(h1, wf_t, bf_col, lp, pad)
    a = _attn_prompt(zqkv, c_t, who, nb, lp, nh, dh)
    d = _pool_prompt(z_rest, pw, lp, pad)
    mix = _poolmix(d, wpm, z_rest, a, pscale, pw)
    mm = _matmul(mix, wo)
    x2, h2 = _post_mix_prompt(mm, head, x_prompt, g2, g3, lp)
    gact, tails = _ffn_gu_prompt(h2, wg, wu, cw, cb, tf, lp, pad)
    f = _matmul(gact, wd, tm_cap=704, x_buffers=1, name="ffn_down")
    y_prompt = _post_ffn_prompt(f, x2, g4, nb, seq, lp, head.shape[0])

    k_prompt = zqkv[:, aw:2 * aw].reshape(nb, lp, nh, dh)[:, pad:][None]
    v_prompt = zqkv[:, 2 * aw:].reshape(nb, lp, nh, dh)[:, pad:][None]
    logf_prompt = lf_t.T.reshape(nb, lp, nh)[:, pad:][None]
    pool_prompt = z_rest[:, :pw].reshape(nb, lp, pw)[:, lp - n_state:][None]
    tiles_per_seq = tails.shape[0] // nb
    conv_prompt = tails.reshape(nb, tiles_per_seq, CONV_TAIL, dff)[:, -1, CONV_TAIL - 2:][None]

    xs = x_sample.reshape(db, dm)
    h1s = _rmsnorm(xs, g1, BF16)
    zqkv_s = _matmul(h1s, w_in_bf, n_cols=3 * aw)
    z_rest_s = _matmul(h1s, w_rest)
    fl_s = _matmul(h1s, wf_pad)
    o_s, lf_s = _attn_sample(page_table, zqkv_s, fl_s, bf_pad, cache_k, cache_v, cache_logf[0], nh, dh)
    a_s = _head_out(o_s, who)
    d_s = _pool_sample(state_pool[0].transpose(1, 0, 2), z_rest_s, pw, past_len)
    mix_s = _poolmix(d_s, wpm, z_rest_s, a_s, pscale, pw)
    mm_s = _matmul(mix_s, wo)
    x2s, h2s = _post_mix(mm_s, xs, g2, g3)
    gact_s, a_new = _ffn_gu_sample(h2s, wg, wu, cw, cb, tf, state_conv[0][:, 0], state_conv[0][:, 1])
    f_s = _matmul(gact_s, wd, name="ffn_down")
    y_s = _post_ffn(f_s, x2s, g4)

    y_sample = y_s.reshape(db, 1, dm)
    k_sample = zqkv_s[:, aw:2 * aw].reshape(1, db, 1, nh, dh)
    v_sample = zqkv_s[:, 2 * aw:].reshape(1, db, 1, nh, dh)
    logf_sample = lf_s.reshape(1, db, 1, nh)
    u_s = z_rest_s[:, :pw]
    pool_sample = jnp.concatenate([state_pool[0][:, 1:], u_s[:, None]], axis=1)[None]
    conv_sample = jnp.stack([state_conv[0][:, 1], a_new], axis=1)[None]

    return (y_prompt, y_sample, k_prompt, v_prompt, logf_prompt, pool_prompt, conv_prompt,
            k_sample, v_sample, logf_sample, pool_sample, conv_sample)
```

```python
import functools
import math

import jax
import jax.numpy as jnp
from jax import lax
from jax.experimental import pallas as pl
from jax.experimental.pallas import tpu as pltpu

F32 = jnp.float32
BF16 = jnp.bfloat16
RMS_EPS = 1e-6
POOL_WINDOWS = (2, 4, 8, 16)
LANES = 128
BF16_SUBLANES = 16
HALO = 16
CONV_TAIL = 8
MASKED_CUMLOG = 1e30
LOG2E = math.log2(math.e)
HEADS_PER_STEP = 4
VMEM_LIMIT = 56 * 1024 * 1024


def _pick_tile(n, cap, mult):
    best = None
    for t in range(mult, min(n, cap) + 1, mult):
        if n % t == 0:
            best = t
    return n if best is None else best


def _params(*sem):
    return pltpu.CompilerParams(dimension_semantics=sem, vmem_limit_bytes=VMEM_LIMIT)


def _rms(x, g):
    return x * lax.rsqrt(jnp.mean(x * x, axis=-1, keepdims=True) + RMS_EPS) * g


def _sigmoid(x):
    return 1.0 / (1.0 + jnp.exp(-x))


def _log_sigmoid(x):
    return jnp.minimum(x, 0.0) - jnp.log(1.0 + jnp.exp(-jnp.abs(x)))


def _gelu_tanh(x):
    cdf = 0.5 * (1.0 + jnp.tanh(math.sqrt(2.0 / math.pi) * (x + 0.044715 * (x * x * x))))
    return x * cdf


def _split3(x):
    hi = x.astype(BF16)
    r = x - hi.astype(F32)
    mid = r.astype(BF16)
    lo = (r - mid.astype(F32)).astype(BF16)
    return hi, mid, lo


_NN = (((1,), (0,)), ((), ()))
_NT = (((1,), (1,)), ((), ()))


def _dot_f32_lhs(x, m_bf, dims):
    out = None
    for piece in _split3(x):
        t = lax.dot_general(piece, m_bf, dims, preferred_element_type=F32)
        out = t if out is None else out + t
    return out


def _dot_f32_rhs(m_bf, x, dims):
    out = None
    for piece in _split3(x):
        t = lax.dot_general(m_bf, piece, dims, preferred_element_type=F32)
        out = t if out is None else out + t
    return out


def _rmsnorm_kernel(x_ref, g_ref, o_ref):
    o_ref[...] = _rms(x_ref[...], g_ref[...]).astype(o_ref.dtype)


def _rmsnorm(x, g, out_dtype):
    m, d = x.shape
    tr = _pick_tile(m, 256, BF16_SUBLANES)
    return pl.pallas_call(
        _rmsnorm_kernel,
        grid=(m // tr,),
        in_specs=[pl.BlockSpec((tr, d), lambda i: (i, 0)), pl.BlockSpec((1, d), lambda i: (0, 0))],
        out_specs=pl.BlockSpec((tr, d), lambda i: (i, 0)),
        out_shape=jax.ShapeDtypeStruct((m, d), out_dtype),
        compiler_params=_params("parallel"),
        name="rmsnorm",
    )(x, g)


def _post_mix_kernel(m_ref, x_ref, g1_ref, g2_ref, x2_ref, h2_ref):
    x2 = x_ref[...] + _rms(m_ref[...], g1_ref[...])
    x2_ref[...] = x2
    h2_ref[...] = _rms(x2, g2_ref[...]).astype(h2_ref.dtype)


def _post_mix(mm, x, g1, g2):
    m, d = x.shape
    tr = _pick_tile(m, 256, BF16_SUBLANES)
    row = pl.BlockSpec((tr, d), lambda i: (i, 0))
    vec = pl.BlockSpec((1, d), lambda i: (0, 0))
    return pl.pallas_call(
        _post_mix_kernel,
        grid=(m // tr,),
        in_specs=[row, row, vec, vec],
        out_specs=[row, row],
        out_shape=[jax.ShapeDtypeStruct((m, d), F32), jax.ShapeDtypeStruct((m, d), BF16)],
        compiler_params=_params("parallel"),
        name="post_mix",
    )(mm, x, g1, g2)


def _post_ffn_kernel(f_ref, x_ref, g_ref, y_ref):
    y_ref[...] = (x_ref[...] + _rms(f_ref[...], g_ref[...])).reshape(y_ref.shape)


def _post_ffn(f, x, g):
    m, d = x.shape
    tr = _pick_tile(m, 256, 8)
    row = pl.BlockSpec((tr, d), lambda i: (i, 0))
    vec = pl.BlockSpec((1, d), lambda i: (0, 0))
    return pl.pallas_call(
        _post_ffn_kernel,
        grid=(m // tr,),
        in_specs=[row, row, vec],
        out_specs=row,
        out_shape=jax.ShapeDtypeStruct((m, d), F32),
        compiler_params=_params("parallel"),
        name="post_ffn",
    )(f, x, g)


def _prompt_x(head_ref, x_ref):
    return jnp.where(pl.program_id(1) == 0, head_ref[...], x_ref[0])


def _rmsnorm_prompt_kernel(head_ref, x_ref, g_ref, o_ref):
    o_ref[...] = _rms(_prompt_x(head_ref, x_ref), g_ref[...]).astype(o_ref.dtype)


def _post_mix_prompt_kernel(m_ref, head_ref, x_ref, g1_ref, g2_ref, x2_ref, h2_ref, *, pad):
    x2 = _prompt_x(head_ref, x_ref) + _rms(m_ref[...], g1_ref[...])
    x2_ref[...] = x2
    n_before = jnp.where(pl.program_id(1) == 0, pad, 0)
    before = lax.broadcasted_iota(jnp.int32, (x2.shape[0], 1), 0) < n_before
    h2_ref[...] = jnp.where(before, 0.0, _rms(x2, g2_ref[...])).astype(h2_ref.dtype)


def _prompt_specs(tr, d, tiles):
    head = pl.BlockSpec((tr, d), lambda b, t: (0, 0))
    x = pl.BlockSpec((1, tr, d), lambda b, t: (b, jnp.maximum(t - 1, 0), 0))
    row = pl.BlockSpec((tr, d), lambda b, t: (b * tiles + t, 0))
    vec = pl.BlockSpec((1, d), lambda b, t: (0, 0))
    return head, x, row, vec


def _rmsnorm_prompt(head, x_prompt, g, lp):
    nb, _, d = x_prompt.shape
    tr = head.shape[0]
    tiles = lp // tr
    head_s, x_s, row, vec = _prompt_specs(tr, d, tiles)
    return pl.pallas_call(
        _rmsnorm_prompt_kernel,
        grid=(nb, tiles),
        in_specs=[head_s, x_s, vec],
        out_specs=row,
        out_shape=jax.ShapeDtypeStruct((nb * lp, d), BF16),
        compiler_params=_params("parallel", "arbitrary"),
        name="rmsnorm_prompt",
    )(head, x_prompt, g)


def _post_mix_prompt(mm, head, x_prompt, g1, g2, lp, pad):
    nb, _, d = x_prompt.shape
    tr = head.shape[0]
    tiles = lp // tr
    head_s, x_s, row, vec = _prompt_specs(tr, d, tiles)
    return pl.pallas_call(
        functools.partial(_post_mix_prompt_kernel, pad=pad),
        grid=(nb, tiles),
        in_specs=[row, head_s, x_s, vec, vec],
        out_specs=[row, row],
        out_shape=[jax.ShapeDtypeStruct((nb * lp, d), F32), jax.ShapeDtypeStruct((nb * lp, d), BF16)],
        compiler_params=_params("parallel", "arbitrary"),
        name="post_mix_prompt",
    )(mm, head, x_prompt, g1, g2)


def _post_ffn_prompt(f, x2, g, nb, seq, lp, tr):
    d = x2.shape[1]
    tiles = lp // tr
    skip = tiles - seq // tr
    row = pl.BlockSpec((tr, d), lambda b, t: (b * tiles + skip + t, 0))
    return pl.pallas_call(
        _post_ffn_kernel,
        grid=(nb, seq // tr),
        in_specs=[row, row, pl.BlockSpec((1, d), lambda b, t: (0, 0))],
        out_specs=pl.BlockSpec((1, tr, d), lambda b, t: (b, t, 0)),
        out_shape=jax.ShapeDtypeStruct((nb, seq, d), F32),
        compiler_params=_params("parallel", "arbitrary"),
        name="post_ffn_prompt",
    )(f, x2, g)


def _mm_kernel(x_ref, w_ref, o_ref):
    o_ref[...] = jnp.dot(x_ref[...], w_ref[...], preferred_element_type=F32).astype(o_ref.dtype)


def _matmul(x, w, out_dtype=F32, tm_cap=1408, tn_cap=512, x_buffers=2, n_cols=None, name="matmul"):
    m, k = x.shape
    n = w.shape[1] if n_cols is None else n_cols
    tm = _pick_tile(m, tm_cap, BF16_SUBLANES)
    tn = _pick_tile(n, tn_cap, LANES)
    x_mode = {} if x_buffers == 2 else {"pipeline_mode": pl.Buffered(x_buffers)}
    x_spec = pl.BlockSpec((tm, k), lambda i, j: (i, 0), **x_mode)
    return pl.pallas_call(
        _mm_kernel,
        grid=(m // tm, n // tn),
        in_specs=[x_spec, pl.BlockSpec((k, tn), lambda i, j: (0, j))],
        out_specs=pl.BlockSpec((tm, tn), lambda i, j: (i, j)),
        out_shape=jax.ShapeDtypeStruct((m, n), out_dtype),
        compiler_params=_params("parallel", "arbitrary"),
        name=name,
    )(x, w)


def _forget_kernel(h_ref, wf_ref, b_ref, lf_ref, c_ref, carry_ref, *, tc, lp, pad):
    i = pl.program_id(0)
    off = lax.rem(i * tc, lp)
    fl = lax.dot_general(wf_ref[...], h_ref[...], _NT, preferred_element_type=F32)
    lf = _log_sigmoid(fl + b_ref[...])
    pos = off + lax.broadcasted_iota(jnp.int32, (1, tc), 1)
    valid = pos >= pad
    lfm = jnp.where(valid, lf, 0.0)
    upper = (lax.broadcasted_iota(jnp.int32, (tc, tc), 0)
             <= lax.broadcasted_iota(jnp.int32, (tc, tc), 1)).astype(BF16)
    cs = _dot_f32_lhs(lfm, upper, _NN)

    @pl.when(off == 0)
    def _():
        carry_ref[...] = jnp.zeros_like(carry_ref)

    c = cs + carry_ref[:, 0:1]
    carry_ref[...] = jnp.broadcast_to(c[:, tc - 1:tc], carry_ref.shape)
    lf_ref[...] = lf
    c_ref[...] = jnp.where(valid, c * LOG2E, MASKED_CUMLOG)


def _forget_prompt(h, wf_t, b_col, lp, pad):
    mp, d = h.shape
    nh = wf_t.shape[0]
    tc = _pick_tile(lp, 384, LANES)
    kern = functools.partial(_forget_kernel, tc=tc, lp=lp, pad=pad)
    return pl.pallas_call(
        kern,
        grid=(mp // tc,),
        in_specs=[pl.BlockSpec((tc, d), lambda i: (i, 0)),
                  pl.BlockSpec((nh, d), lambda i: (0, 0)),
                  pl.BlockSpec((nh, 1), lambda i: (0, 0))],
        out_specs=[pl.BlockSpec((nh, tc), lambda i: (0, i)), pl.BlockSpec((nh, tc), lambda i: (0, i))],
        out_shape=[jax.ShapeDtypeStruct((nh, mp), F32), jax.ShapeDtypeStruct((nh, mp), F32)],
        scratch_shapes=[pltpu.VMEM((nh, LANES), F32)],
        compiler_params=_params("arbitrary"),
        name="forget_prompt",
    )(h, wf_t, b_col)


def _attn_kernel(q_ref, k_ref, v_ref, c_ref, who_ref, a_ref, kb_ref, vb_ref, *, t, dh, scale):
    iq = pl.program_id(2)
    hps = who_ref.shape[0]
    ho = who_ref.shape[2]

    @pl.when(iq == 0)
    def _():
        kb_ref[...] = k_ref[...].astype(BF16)
        vb_ref[...] = v_ref[...].astype(BF16)

    qs = [q_ref[:, hh * dh:(hh + 1) * dh].astype(BF16) for hh in range(hps)]
    rows = iq * t + lax.broadcasted_iota(jnp.int32, (t, 1), 0)

    def step(j, carry, diagonal):
        start = pl.multiple_of(j * t, t)
        out = []
        for hh in range(hps):
            m, l, acc = carry[hh]
            ks = kb_ref[pl.ds(start, t), hh * dh:(hh + 1) * dh]
            vs = vb_ref[pl.ds(start, t), hh * dh:(hh + 1) * dh]
            s = lax.dot_general(qs[hh], ks, _NT, preferred_element_type=F32)
            tt = s * (scale * LOG2E) - c_ref[hh, j]
            if diagonal:
                cols = j * t + lax.broadcasted_iota(jnp.int32, (1, t), 1)
                tt = jnp.where(cols <= rows, tt, -jnp.inf)
            m_new = jnp.maximum(m, jnp.max(tt, axis=-1, keepdims=True))
            alpha = jnp.exp2(m - m_new)
            p = jnp.exp2(tt - m_new)
            l = alpha * l + jnp.sum(p, axis=-1, keepdims=True)
            acc = alpha * acc + jnp.dot(p.astype(BF16), vs, preferred_element_type=F32)
            out.append((m_new, l, acc))
        return tuple(out)

    one = (jnp.full((t, 1), -jnp.inf, F32), jnp.zeros((t, 1), F32), jnp.zeros((t, dh), F32))
    carry = lax.fori_loop(0, iq, functools.partial(step, diagonal=False), (one,) * hps)
    carry = step(iq, carry, True)
    for hh in range(hps):
        _, l, acc = carry[hh]
        o = (acc / l).astype(BF16)
        a_ref[:, hh * ho:(hh + 1) * ho] = jnp.dot(o, who_ref[hh], preferred_element_type=F32)


def _attn_prompt(zqkv, c_t, who, nb, lp, nh, dh):
    mp = zqkv.shape[0]
    ho = who.shape[2]
    t = _pick_tile(lp, 384, LANES)
    nq = lp // t
    hps = HEADS_PER_STEP
    ng = nh // hps
    c4 = c_t.reshape(nh, mp // t, 1, t)
    kern = functools.partial(_attn_kernel, t=t, dh=dh, scale=dh ** -0.5)
    return pl.pallas_call(
        kern,
        grid=(nb, ng, nq),
        in_specs=[pl.BlockSpec((t, hps * dh), lambda b, h, i: (b * nq + i, h)),
                  pl.BlockSpec((lp, hps * dh), lambda b, h, i: (b, ng + h)),
                  pl.BlockSpec((lp, hps * dh), lambda b, h, i: (b, 2 * ng + h)),
                  pl.BlockSpec((hps, nq, 1, t), lambda b, h, i: (h, b, 0, 0)),
                  pl.BlockSpec((hps, dh, ho), lambda b, h, i: (h, 0, 0))],
        out_specs=pl.BlockSpec((t, hps * ho), lambda b, h, i: (b * nq + i, h)),
        out_shape=jax.ShapeDtypeStruct((mp, nh * ho), F32),
        scratch_shapes=[pltpu.VMEM((lp, hps * dh), BF16), pltpu.VMEM((lp, hps * dh), BF16)],
        compiler_params=_params("parallel", "parallel", "arbitrary"),
        name="attn_prompt",
    )(zqkv, zqkv, zqkv, c4, who)


def _softmax_update(tiles, m_ref, l_ref, acc_ref, pv):
    mt = tiles[0]
    for t in tiles[1:]:
        mt = jnp.maximum(mt, t)
    m_old = m_ref[:, 0:1]
    m_new = jnp.maximum(m_old, jnp.max(mt, axis=1, keepdims=True))
    alpha = jnp.exp(m_old - m_new)
    ps = [jnp.exp(t - m_new) for t in tiles]
    tot = ps[0]
    for p in ps[1:]:
        tot = tot + p
    l_ref[...] = jnp.broadcast_to(alpha * l_ref[:, 0:1] + jnp.sum(tot, axis=1, keepdims=True), l_ref.shape)
    acc_ref[...] = alpha * acc_ref[...] + pv(ps)
    m_ref[...] = jnp.broadcast_to(m_new, m_ref.shape)


def _sattn_kernel(pt_ref, q_ref, kn_ref, vn_ref, fl_ref, bf_ref, w2_ref, *refs, nh, dh, ps, gp, scale):
    del pt_ref
    k_refs, v_refs, lf_refs = refs[0:gp], refs[gp:2 * gp], refs[2 * gp:3 * gp]
    o_ref, lfo_ref = refs[3 * gp], refs[3 * gp + 1]
    m_ref, l_ref, carry_ref, acc_ref, lfs_ref = refs[3 * gp + 2:]
    step = pl.program_id(1)
    rows = ps * nh
    nchunk = rows // LANES
    lane = lax.broadcasted_iota(jnp.int32, (nh, LANES), 1)
    diag = (lane & (nh - 1)) == lax.broadcasted_iota(jnp.int32, (nh, LANES), 0)

    @pl.when(step == 0)
    def _():
        m_ref[...] = jnp.full_like(m_ref, -jnp.inf)
        l_ref[...] = jnp.zeros_like(l_ref)
        carry_ref[...] = jnp.zeros_like(carry_ref)
        acc_ref[...] = jnp.zeros_like(acc_ref)

    qb = q_ref[0].astype(BF16)

    for g in range(gp):
        lfs_ref[g:g + 1, :] = lf_refs[g][0]
    x = lfs_ref[...]
    w2 = w2_ref[...]
    run = jnp.zeros((gp, LANES), F32)
    cks = []
    for k in range(nchunk):
        r = _dot_f32_lhs(x[:, k * LANES:(k + 1) * LANES], w2, _NN)
        cks.append(r[:, :LANES] + run)
        run = run + r[:, LANES:]
    earlier = (lax.broadcasted_iota(jnp.int32, (gp, gp), 0)
               > lax.broadcasted_iota(jnp.int32, (gp, gp), 1)).astype(BF16)
    offs = _dot_f32_rhs(earlier, run, _NN) + carry_ref[0:1, :]
    carry_new = offs[gp - 1:gp] + run[gp - 1:gp]

    tiles = []
    for g in range(gp):
        kf = k_refs[g][0, 0].reshape(rows, dh).astype(BF16)
        zt = lax.dot_general(qb, kf, _NT, preferred_element_type=F32)
        for k in range(nchunk):
            c_row = cks[k][g:g + 1] + offs[g:g + 1]
            tiles.append(jnp.where(diag, zt[:, k * LANES:(k + 1) * LANES] * scale - c_row, -jnp.inf))

    def pv(probs):
        out = None
        for g in range(gp):
            pm = jnp.concatenate(probs[g * nchunk:(g + 1) * nchunk], axis=1).astype(BF16)
            vf = v_refs[g][0, 0].reshape(rows, dh).astype(BF16)
            o = jnp.dot(pm, vf, preferred_element_type=F32)
            out = o if out is None else out + o
        return out

    _softmax_update(tiles, m_ref, l_ref, acc_ref, pv)
    carry_ref[...] = jnp.broadcast_to(carry_new, carry_ref.shape)

    @pl.when(step == pl.num_programs(1) - 1)
    def _():
        lf_new = _log_sigmoid(fl_ref[0] + bf_ref[...])
        lfo_ref[0] = lf_new
        c_new = carry_new + lf_new
        zeros = jnp.zeros((LANES - nh, dh), F32)
        kpad = jnp.concatenate([kn_ref[0], zeros], axis=0).astype(BF16)
        vpad = jnp.concatenate([vn_ref[0], zeros], axis=0).astype(BF16)
        zt = lax.dot_general(qb, kpad, _NT, preferred_element_type=F32)
        tile = jnp.where(diag & (lane < nh), zt * scale - c_new, -jnp.inf)
        _softmax_update([tile], m_ref, l_ref, acc_ref,
                        lambda probs: jnp.dot(probs[0].astype(BF16), vpad, preferred_element_type=F32))
        o_ref[0] = acc_ref[...] / l_ref[:, 0:1]


def _attn_sample(page_table, zqkv_s, fl_s, bf_pad, cache_k, cache_v, cache_lf, nh, dh):
    db, n_pages = page_table.shape
    n_phys, ps = cache_k.shape[1], cache_k.shape[2]
    assert nh & (nh - 1) == 0 and LANES % nh == 0 and nh % 8 == 0
    rows = ps * nh
    gp = _pick_tile(n_pages, 8, 1)
    qkv_h = zqkv_s.reshape(db, 3 * nh, dh)
    fl3 = fl_s.reshape(db, 1, LANES)
    lf_flat = cache_lf.reshape(n_phys, 1, rows)
    r = jnp.arange(LANES)
    same = (r[:, None] % nh) == (r[None, :] % nh)
    w2 = jnp.concatenate([same & (r[:, None] <= r[None, :]), same], axis=1).astype(BF16)
    kern = functools.partial(_sattn_kernel, nh=nh, dh=dh, ps=ps, gp=gp, scale=dh ** -0.5)

    def page_spec(g):
        return pl.BlockSpec((1, 1, ps, nh, dh), lambda b, s, pt: (0, pt[b, s * gp + g], 0, 0, 0))

    def lf_spec(g):
        return pl.BlockSpec((1, 1, rows), lambda b, s, pt: (pt[b, s * gp + g], 0, 0))

    grid_spec = pltpu.PrefetchScalarGridSpec(
        num_scalar_prefetch=1,
        grid=(db, n_pages // gp),
        in_specs=[pl.BlockSpec((1, nh, dh), lambda b, s, pt: (b, 0, 0)),
                  pl.BlockSpec((1, nh, dh), lambda b, s, pt: (b, 1, 0)),
                  pl.BlockSpec((1, nh, dh), lambda b, s, pt: (b, 2, 0)),
                  pl.BlockSpec((1, 1, LANES), lambda b, s, pt: (b, 0, 0)),
                  pl.BlockSpec((1, LANES), lambda b, s, pt: (0, 0)),
                  pl.BlockSpec((LANES, 2 * LANES), lambda b, s, pt: (0, 0))]
                 + [page_spec(g) for g in range(gp)] + [page_spec(g) for g in range(gp)]
                 + [lf_spec(g) for g in range(gp)],
        out_specs=[pl.BlockSpec((1, nh, dh), lambda b, s, pt: (b, 0, 0)),
                   pl.BlockSpec((1, 1, LANES), lambda b, s, pt: (b, 0, 0))],
        scratch_shapes=[pltpu.VMEM((nh, LANES), F32), pltpu.VMEM((nh, LANES), F32),
                        pltpu.VMEM((8, LANES), F32), pltpu.VMEM((nh, dh), F32),
                        pltpu.VMEM((gp, rows), F32)],
    )
    o3, lf3 = pl.pallas_call(
        kern,
        grid_spec=grid_spec,
        out_shape=[jax.ShapeDtypeStruct((db, nh, dh), F32), jax.ShapeDtypeStruct((db, 1, LANES), F32)],
        compiler_params=_params("parallel", "arbitrary"),
        name="attn_sample",
    )(page_table, qkv_h, qkv_h, qkv_h, fl3, bf_pad, w2,
      *([cache_k] * gp), *([cache_v] * gp), *([lf_flat] * gp))
    return o3.reshape(db, nh * dh), lf3.reshape(db, LANES)[:, :nh]


def _head_out_kernel(o_ref, w_ref, a_ref):
    a_ref[...] = jnp.dot(o_ref[...].astype(BF16), w_ref[0], preferred_element_type=F32)


def _head_out(o, who):
    m = o.shape[0]
    nh, dh, ho = who.shape
    return pl.pallas_call(
        _head_out_kernel,
        grid=(nh,),
        in_specs=[pl.BlockSpec((m, dh), lambda h: (0, h)), pl.BlockSpec((1, dh, ho), lambda h: (h, 0, 0))],
        out_specs=pl.BlockSpec((m, ho), lambda h: (0, h)),
        out_shape=jax.ShapeDtypeStruct((m, nh * ho), F32),
        compiler_params=_params("parallel"),
        name="head_out",
    )(o, who)


def _pool_prompt_kernel(halo_ref, u_ref, d_ref, ext_ref, *, tr, lp, pad, pg):
    i = pl.program_id(0)
    ext_ref[0:HALO, :] = halo_ref[...]
    ext_ref[HALO:, :] = u_ref[...]
    pos = lax.rem(i * tr, lp) - pad + lax.broadcasted_iota(jnp.int32, (tr, 1), 0)
    for g, w in enumerate(POOL_WINDOWS):
        cols = slice(g * pg, (g + 1) * pg)
        u = ext_ref[pl.ds(HALO, tr), cols]
        acc = u
        for k in range(1, w):
            acc = acc + ext_ref[pl.ds(HALO - k, tr), cols]
        cnt = jnp.clip(pos + 1, 1, w).astype(F32)
        d_ref[:, cols] = (acc / cnt - u).astype(d_ref.dtype)


def _pool_prompt(z_rest, pw, lp, pad):
    mp = z_rest.shape[0]
    tr = _pick_tile(lp, 384, BF16_SUBLANES)
    kern = functools.partial(_pool_prompt_kernel, tr=tr, lp=lp, pad=pad, pg=pw // len(POOL_WINDOWS))
    return pl.pallas_call(
        kern,
        grid=(mp // tr,),
        in_specs=[pl.BlockSpec((HALO, pw), lambda i: (jnp.maximum(i * (tr // HALO) - 1, 0), 0)),
                  pl.BlockSpec((tr, pw), lambda i: (i, 0))],
        out_specs=pl.BlockSpec((tr, pw), lambda i: (i, 0)),
        out_shape=jax.ShapeDtypeStruct((mp, pw), BF16),
        scratch_shapes=[pltpu.VMEM((tr + HALO, pw), F32)],
        compiler_params=_params("parallel"),
        name="pool_prompt",
    )(z_rest, z_rest)


def _pool_sample_kernel(st_ref, u_ref, d_ref, *, cnts, pg):
    u_all = u_ref[...]
    n_state = st_ref.shape[0]
    for g, cnt in enumerate(cnts):
        cols = slice(g * pg, (g + 1) * pg)
        u = u_all[:, cols]
        acc = u
        for k in range(1, cnt):
            acc = acc + st_ref[n_state - k, :, cols]
        d_ref[:, cols] = (acc / float(cnt) - u).astype(d_ref.dtype)


def _pool_sample(state_t, z_rest_s, pw, past_len):
    n_state, db, _ = state_t.shape
    tb = _pick_tile(db, 16, BF16_SUBLANES)
    cnts = tuple(min(past_len + 1, w) for w in POOL_WINDOWS)
    kern = functools.partial(_pool_sample_kernel, cnts=cnts, pg=pw // len(POOL_WINDOWS))
    return pl.pallas_call(
        kern,
        grid=(db // tb,),
        in_specs=[pl.BlockSpec((n_state, tb, pw), lambda i: (0, i, 0)),
                  pl.BlockSpec((tb, pw), lambda i: (i, 0))],
        out_specs=pl.BlockSpec((tb, pw), lambda i: (i, 0)),
        out_shape=jax.ShapeDtypeStruct((db, pw), BF16),
        compiler_params=_params("parallel"),
        name="pool_sample",
    )(state_t, z_rest_s)


def _poolmix_kernel(d_ref, w_ref, ga_ref, gp_ref, a_ref, ps_ref, o_ref):
    p = jnp.dot(d_ref[...], w_ref[0], preferred_element_type=F32) * ps_ref[...]
    o_ref[...] = (_sigmoid(ga_ref[...]) * a_ref[...] + _sigmoid(gp_ref[...]) * p).astype(o_ref.dtype)


def _poolmix(d, wpm, z_rest, a, pool_scale, pw):
    m, dm = a.shape
    ng, pg, pgo = wpm.shape
    tm = _pick_tile(m, 1056, BF16_SUBLANES)
    tn = pgo
    assert pw % tn == 0 and dm % tn == 0
    ga0 = pw // tn
    gp0 = (pw + dm) // tn
    return pl.pallas_call(
        _poolmix_kernel,
        grid=(m // tm, ng),
        in_specs=[pl.BlockSpec((tm, pg), lambda i, g: (i, g)),
                  pl.BlockSpec((1, pg, tn), lambda i, g: (g, 0, 0)),
                  pl.BlockSpec((tm, tn), lambda i, g: (i, ga0 + g)),
                  pl.BlockSpec((tm, tn), lambda i, g: (i, gp0 + g)),
                  pl.BlockSpec((tm, tn), lambda i, g: (i, g)),
                  pl.BlockSpec((1, tn), lambda i, g: (0, g))],
        out_specs=pl.BlockSpec((tm, tn), lambda i, g: (i, g)),
        out_shape=jax.ShapeDtypeStruct((m, dm), BF16),
        compiler_params=_params("parallel", "arbitrary"),
        name="poolmix",
    )(d, wpm, z_rest, z_rest, a, pool_scale)


def _conv_gate(a, a1, a2, b, cw_ref, cb_ref):
    cw = cw_ref[...]
    conv = cb_ref[...] + cw[0:1] * a2 + cw[1:2] * a1 + cw[2:3] * a
    return (_gelu_tanh(conv) * b).astype(BF16)


def _ffn_gu_prompt_kernel(h_ref, wg_ref, wu_ref, cw_ref, cb_ref, g_ref, tail_ref, aext_ref, tails_ref,
                          *, tm, tf):
    i = pl.program_id(0)
    j = pl.program_id(1)
    h = h_ref[...]
    a = jnp.dot(h, wg_ref[...], preferred_element_type=F32)
    b = jnp.dot(h, wu_ref[...], preferred_element_type=F32)

    @pl.when(i == 0)
    def _():
        tails_ref[j] = jnp.zeros((CONV_TAIL, tf), F32)

    aext_ref[0:CONV_TAIL, :] = tails_ref[j]
    aext_ref[CONV_TAIL:, :] = a
    a1 = aext_ref[pl.ds(CONV_TAIL - 1, tm), :]
    a2 = aext_ref[pl.ds(CONV_TAIL - 2, tm), :]
    last = a[tm - CONV_TAIL:, :]
    tails_ref[j] = last
    tail_ref[0] = last
    g_ref[...] = _conv_gate(a, a1, a2, b, cw_ref, cb_ref)


def _ffn_gu_sample_kernel(h_ref, wg_ref, wu_ref, cw_ref, cb_ref, s0_ref, s1_ref, g_ref, a_ref):
    h = h_ref[...]
    a = jnp.dot(h, wg_ref[...], preferred_element_type=F32)
    b = jnp.dot(h, wu_ref[...], preferred_element_type=F32)
    a_ref[...] = a
    g_ref[...] = _conv_gate(a, s1_ref[...], s0_ref[...], b, cw_ref, cb_ref)


def _ffn_gu_specs(tm, dm, tf):
    return [pl.BlockSpec((tm, dm), lambda i, j: (i, 0)),
            pl.BlockSpec((dm, tf), lambda i, j: (0, j)),
            pl.BlockSpec((dm, tf), lambda i, j: (0, j)),
            pl.BlockSpec((3, tf), lambda i, j: (0, j)),
            pl.BlockSpec((1, tf), lambda i, j: (0, j))]


def _ffn_gu_prompt(h2, wg, wu, cw, cb, tf, lp):
    mp, dm = h2.shape
    nj = wg.shape[1] // tf
    tm = _pick_tile(lp, 1408, BF16_SUBLANES)
    kern = functools.partial(_ffn_gu_prompt_kernel, tm=tm, tf=tf)
    return pl.pallas_call(
        kern,
        grid=(mp // tm, nj),
        in_specs=_ffn_gu_specs(tm, dm, tf),
        out_specs=[pl.BlockSpec((tm, tf), lambda i, j: (i, j)),
                   pl.BlockSpec((1, CONV_TAIL, tf), lambda i, j: (i, 0, j))],
        out_shape=[jax.ShapeDtypeStruct((mp, nj * tf), BF16),
                   jax.ShapeDtypeStruct((mp // tm, CONV_TAIL, nj * tf), F32)],
        scratch_shapes=[pltpu.VMEM((tm + CONV_TAIL, tf), F32), pltpu.VMEM((nj, CONV_TAIL, tf), F32)],
        compiler_params=_params("arbitrary", "arbitrary"),
        name="ffn_gu_prompt",
    )(h2, wg, wu, cw, cb)


def _ffn_gu_sample(h2, wg, wu, cw, cb, tf, s0, s1):
    m, dm = h2.shape
    nj = wg.shape[1] // tf
    kern = _ffn_gu_sample_kernel
    col = pl.BlockSpec((m, tf), lambda i, j: (0, j))
    return pl.pallas_call(
        kern,
        grid=(1, nj),
        in_specs=_ffn_gu_specs(m, dm, tf) + [col, col],
        out_specs=[col, col],
        out_shape=[jax.ShapeDtypeStruct((m, nj * tf), BF16), jax.ShapeDtypeStruct((m, nj * tf), F32)],
        compiler_params=_params("arbitrary", "arbitrary"),
        name="ffn_gu_sample",
    )(h2, wg, wu, cw, cb, s0, s1)


def kernel(x_prompt, x_sample, cache_k, cache_v, cache_logf, state_pool, state_conv, page_table, meta_tokens, g_pre_mix, g_post_mix, g_pre_ffn, g_post_ffn, w_in, b_forget, w_head_out, w_pool_map, pool_scale, w_out, w_gate, w_up, conv_w, conv_b, w_down):
    nb, seq, dm = x_prompt.shape
    db = x_sample.shape[0]
    depth, _, ps, nh, dh = cache_k.shape
    assert depth == 1 and x_sample.shape[1] == 1
    n_meta = meta_tokens.shape[0]
    n_state, pw = state_pool.shape[2], state_pool.shape[3]
    dff = w_gate.shape[2]
    aw = nh * dh
    past_len = page_table.shape[1] * ps
    assert n_state <= HALO - 1 and w_in.shape[2] == 3 * aw + nh + pw + 2 * dm

    pad = (-(n_meta + seq)) % LANES
    if pad < HALO:
        pad += LANES
    lp = pad + n_meta + seq
    mp = nb * lp

    w_in_bf = w_in[0].astype(BF16)
    w_f = w_in[0, :, 3 * aw:3 * aw + nh]
    w_rest = w_in_bf[:, 3 * aw + nh:]
    wf_t = w_f.T.astype(BF16)
    wf_pad = jnp.pad(w_f, ((0, 0), (0, LANES - nh))).astype(BF16)
    who = w_head_out[0].astype(BF16)
    wpm = w_pool_map[0].astype(BF16)
    wo = w_out[0].astype(BF16)
    tf = _pick_tile(dff, 256, LANES)
    wg = w_gate[0].astype(BF16)
    wu = w_up[0].astype(BF16)
    wd = w_down[0].astype(BF16)
    cw = conv_w[0]
    cb = conv_b[0].reshape(1, dff)
    g1, g2, g3, g4 = (g[0].reshape(1, dm) for g in (g_pre_mix, g_post_mix, g_pre_ffn, g_post_ffn))
    pscale = pool_scale[0].reshape(1, dm)
    bf_col = b_forget[0].reshape(nh, 1)
    bf_pad = jnp.pad(b_forget[0].reshape(1, nh), ((0, 0), (0, LANES - nh)))

    head = jnp.concatenate([jnp.zeros((pad, dm), F32), meta_tokens], axis=0)
    assert seq % head.shape[0] == 0
    h1 = _rmsnorm_prompt(head, x_prompt, g1, lp)
    zqkv = _matmul(h1, w_in_bf, n_cols=3 * aw)
    z_rest = _matmul(h1, w_rest)
    lf_t, c_t = _forget_prompt(h1, wf_t, bf_col, lp, pad)
    a = _attn_prompt(zqkv, c_t, who, nb, lp, nh, dh)
    d = _pool_prompt(z_rest, pw, lp, pad)
    mix = _poolmix(d, wpm, z_rest, a, pscale, pw)
    mm = _matmul(mix, wo)
    x2, h2 = _post_mix_prompt(mm, head, x_prompt, g2, g3, lp, pad)
    gact, tails = _ffn_gu_prompt(h2, wg, wu, cw, cb, tf, lp)
    f = _matmul(gact, wd, tm_cap=704, x_buffers=1, name="ffn_down")
    y_prompt = _post_ffn_prompt(f, x2, g4, nb, seq, lp, head.shape[0])

    k_prompt = zqkv[:, aw:2 * aw].reshape(nb, lp, nh, dh)[:, pad:][None]
    v_prompt = zqkv[:, 2 * aw:].reshape(nb, lp, nh, dh)[:, pad:][None]
    logf_prompt = lf_t.T.reshape(nb, lp, nh)[:, pad:][None]
    pool_prompt = z_rest.reshape(nb, lp, -1)[:, lp - n_state:, :pw][None]
    tiles_per_seq = tails.shape[0] // nb
    conv_prompt = tails.reshape(nb, tiles_per_seq, CONV_TAIL, dff)[:, -1, CONV_TAIL - 2:][None]

    xs = x_sample.reshape(db, dm)
    h1s = _rmsnorm(xs, g1, BF16)
    zqkv_s = _matmul(h1s, w_in_bf, n_cols=3 * aw)
    z_rest_s = _matmul(h1s, w_rest)
    fl_s = _matmul(h1s, wf_pad)
    o_s, lf_s = _attn_sample(page_table, zqkv_s, fl_s, bf_pad, cache_k, cache_v, cache_logf[0], nh, dh)
    a_s = _head_out(o_s, who)
    d_s = _pool_sample(state_pool[0].transpose(1, 0, 2), z_rest_s, pw, past_len)
    mix_s = _poolmix(d_s, wpm, z_rest_s, a_s, pscale, pw)
    mm_s = _matmul(mix_s, wo)
    x2s, h2s = _post_mix(mm_s, xs, g2, g3)
    gact_s, a_new = _ffn_gu_sample(h2s, wg, wu, cw, cb, tf, state_conv[0][:, 0], state_conv[0][:, 1])
    f_s = _matmul(gact_s, wd, name="ffn_down")
    y_s = _post_ffn(f_s, x2s, g4)

    y_sample = y_s.reshape(db, 1, dm)
    k_sample = zqkv_s[:, aw:2 * aw].reshape(1, db, 1, nh, dh)
    v_sample = zqkv_s[:, 2 * aw:].reshape(1, db, 1, nh, dh)
    logf_sample = lf_s.reshape(1, db, 1, nh)
    u_s = z_rest_s[:, :pw]
    pool_sample = jnp.concatenate([state_pool[0][:, 1:], u_s[:, None]], axis=1)[None]
    conv_sample = jnp.stack([state_conv[0][:, 1], a_new], axis=1)[None]

    return (y_prompt, y_sample, k_prompt, v_prompt, logf_prompt, pool_prompt, conv_prompt,
            k_sample, v_sample, logf_sample, pool_sample, conv_sample)
```

```python
import functools
import math

import jax
import jax.numpy as jnp
from jax import lax
from jax.experimental import pallas as pl
from jax.experimental.pallas import tpu as pltpu

F32 = jnp.float32
BF16 = jnp.bfloat16
RMS_EPS = 1e-6
POOL_WINDOWS = (2, 4, 8, 16)
LANES = 128
BF16_SUBLANES = 16
HALO = 16
CONV_TAIL = 8
MASKED_CUMLOG = 1e30
LOG2E = math.log2(math.e)
HEADS_PER_STEP = 4
VMEM_LIMIT = 56 * 1024 * 1024


def _pick_tile(n, cap, mult):
    best = None
    for t in range(mult, min(n, cap) + 1, mult):
        if n % t == 0:
            best = t
    return n if best is None else best


def _params(*sem):
    return pltpu.CompilerParams(dimension_semantics=sem, vmem_limit_bytes=VMEM_LIMIT)


def _rms(x, g):
    return x * lax.rsqrt(jnp.mean(x * x, axis=-1, keepdims=True) + RMS_EPS) * g


def _sigmoid(x):
    return 1.0 / (1.0 + jnp.exp(-x))


def _log_sigmoid(x):
    return jnp.minimum(x, 0.0) - jnp.log(1.0 + jnp.exp(-jnp.abs(x)))


def _gelu_tanh(x):
    cdf = 0.5 * (1.0 + jnp.tanh(math.sqrt(2.0 / math.pi) * (x + 0.044715 * (x * x * x))))
    return x * cdf


def _split3(x):
    hi = x.astype(BF16)
    r = x - hi.astype(F32)
    mid = r.astype(BF16)
    lo = (r - mid.astype(F32)).astype(BF16)
    return hi, mid, lo


_NN = (((1,), (0,)), ((), ()))
_NT = (((1,), (1,)), ((), ()))


def _dot_f32_lhs(x, m_bf, dims):
    out = None
    for piece in _split3(x):
        t = lax.dot_general(piece, m_bf, dims, preferred_element_type=F32)
        out = t if out is None else out + t
    return out


def _dot_f32_rhs(m_bf, x, dims):
    out = None
    for piece in _split3(x):
        t = lax.dot_general(m_bf, piece, dims, preferred_element_type=F32)
        out = t if out is None else out + t
    return out


def _rmsnorm_kernel(x_ref, g_ref, o_ref):
    o_ref[...] = _rms(x_ref[...], g_ref[...]).astype(o_ref.dtype)


def _rmsnorm(x, g, out_dtype):
    m, d = x.shape
    tr = _pick_tile(m, 256, BF16_SUBLANES)
    return pl.pallas_call(
        _rmsnorm_kernel,
        grid=(m // tr,),
        in_specs=[pl.BlockSpec((tr, d), lambda i: (i, 0)), pl.BlockSpec((1, d), lambda i: (0, 0))],
        out_specs=pl.BlockSpec((tr, d), lambda i: (i, 0)),
        out_shape=jax.ShapeDtypeStruct((m, d), out_dtype),
        compiler_params=_params("parallel"),
        name="rmsnorm",
    )(x, g)


def _post_mix_kernel(m_ref, x_ref, g1_ref, g2_ref, x2_ref, h2_ref):
    x2 = x_ref[...] + _rms(m_ref[...], g1_ref[...])
    x2_ref[...] = x2
    h2_ref[...] = _rms(x2, g2_ref[...]).astype(h2_ref.dtype)


def _post_mix(mm, x, g1, g2):
    m, d = x.shape
    tr = _pick_tile(m, 256, BF16_SUBLANES)
    row = pl.BlockSpec((tr, d), lambda i: (i, 0))
    vec = pl.BlockSpec((1, d), lambda i: (0, 0))
    return pl.pallas_call(
        _post_mix_kernel,
        grid=(m // tr,),
        in_specs=[row, row, vec, vec],
        out_specs=[row, row],
        out_shape=[jax.ShapeDtypeStruct((m, d), F32), jax.ShapeDtypeStruct((m, d), BF16)],
        compiler_params=_params("parallel"),
        name="post_mix",
    )(mm, x, g1, g2)


def _post_ffn_kernel(f_ref, x_ref, g_ref, y_ref):
    y_ref[...] = (x_ref[...] + _rms(f_ref[...], g_ref[...])).reshape(y_ref.shape)


def _post_ffn(f, x, g):
    m, d = x.shape
    tr = _pick_tile(m, 256, 8)
    row = pl.BlockSpec((tr, d), lambda i: (i, 0))
    vec = pl.BlockSpec((1, d), lambda i: (0, 0))
    return pl.pallas_call(
        _post_ffn_kernel,
        grid=(m // tr,),
        in_specs=[row, row, vec],
        out_specs=row,
        out_shape=jax.ShapeDtypeStruct((m, d), F32),
        compiler_params=_params("parallel"),
        name="post_ffn",
    )(f, x, g)


def _prompt_x(head_ref, x_ref):
    return jnp.where(pl.program_id(1) == 0, head_ref[...], x_ref[0])


def _rmsnorm_prompt_kernel(head_ref, x_ref, g_ref, o_ref):
    o_ref[...] = _rms(_prompt_x(head_ref, x_ref), g_ref[...]).astype(o_ref.dtype)


def _post_mix_prompt_kernel(m_ref, head_ref, x_ref, g1_ref, g2_ref, x2_ref, h2_ref, *, pad):
    x2 = _prompt_x(head_ref, x_ref) + _rms(m_ref[...], g1_ref[...])
    x2_ref[...] = x2
    n_before = jnp.where(pl.program_id(1) == 0, pad, 0)
    before = lax.broadcasted_iota(jnp.int32, (x2.shape[0], 1), 0) < n_before
    h2_ref[...] = jnp.where(before, 0.0, _rms(x2, g2_ref[...])).astype(h2_ref.dtype)


def _prompt_specs(tr, d, tiles):
    head = pl.BlockSpec((tr, d), lambda b, t: (0, 0))
    x = pl.BlockSpec((1, tr, d), lambda b, t: (b, jnp.maximum(t - 1, 0), 0))
    row = pl.BlockSpec((tr, d), lambda b, t: (b * tiles + t, 0))
    vec = pl.BlockSpec((1, d), lambda b, t: (0, 0))
    return head, x, row, vec


def _rmsnorm_prompt(head, x_prompt, g, lp):
    nb, _, d = x_prompt.shape
    tr = head.shape[0]
    tiles = lp // tr
    head_s, x_s, row, vec = _prompt_specs(tr, d, tiles)
    return pl.pallas_call(
        _rmsnorm_prompt_kernel,
        grid=(nb, tiles),
        in_specs=[head_s, x_s, vec],
        out_specs=row,
        out_shape=jax.ShapeDtypeStruct((nb * lp, d), BF16),
        compiler_params=_params("parallel", "arbitrary"),
        name="rmsnorm_prompt",
    )(head, x_prompt, g)


def _post_mix_prompt(mm, head, x_prompt, g1, g2, lp, pad):
    nb, _, d = x_prompt.shape
    tr = head.shape[0]
    tiles = lp // tr
    head_s, x_s, row, vec = _prompt_specs(tr, d, tiles)
    return pl.pallas_call(
        functools.partial(_post_mix_prompt_kernel, pad=pad),
        grid=(nb, tiles),
        in_specs=[row, head_s, x_s, vec, vec],
        out_specs=[row, row],
        out_shape=[jax.ShapeDtypeStruct((nb * lp, d), F32), jax.ShapeDtypeStruct((nb * lp, d), BF16)],
        compiler_params=_params("parallel", "arbitrary"),
        name="post_mix_prompt",
    )(mm, head, x_prompt, g1, g2)


def _post_ffn_prompt(f, x2, g, nb, seq, lp, tr):
    d = x2.shape[1]
    tiles = lp // tr
    skip = tiles - seq // tr
    row = pl.BlockSpec((tr, d), lambda b, t: (b * tiles + skip + t, 0))
    return pl.pallas_call(
        _post_ffn_kernel,
        grid=(nb, seq // tr),
        in_specs=[row, row, pl.BlockSpec((1, d), lambda b, t: (0, 0))],
        out_specs=pl.BlockSpec((1, tr, d), lambda b, t: (b, t, 0)),
        out_shape=jax.ShapeDtypeStruct((nb, seq, d), F32),
        compiler_params=_params("parallel", "arbitrary"),
        name="post_ffn_prompt",
    )(f, x2, g)


def _mm_kernel(x_ref, w_ref, o_ref):
    o_ref[...] = jnp.dot(x_ref[...], w_ref[...], preferred_element_type=F32).astype(o_ref.dtype)


def _matmul(x, w, out_dtype=F32, tm_cap=1408, tn_cap=512, x_buffers=2, n_cols=None, name="matmul"):
    m, k = x.shape
    n = w.shape[1] if n_cols is None else n_cols
    tm = _pick_tile(m, tm_cap, BF16_SUBLANES)
    tn = _pick_tile(n, tn_cap, LANES)
    x_mode = {} if x_buffers == 2 else {"pipeline_mode": pl.Buffered(x_buffers)}
    x_spec = pl.BlockSpec((tm, k), lambda i, j: (i, 0), **x_mode)
    return pl.pallas_call(
        _mm_kernel,
        grid=(m // tm, n // tn),
        in_specs=[x_spec, pl.BlockSpec((k, tn), lambda i, j: (0, j))],
        out_specs=pl.BlockSpec((tm, tn), lambda i, j: (i, j)),
        out_shape=jax.ShapeDtypeStruct((m, n), out_dtype),
        compiler_params=_params("parallel", "arbitrary"),
        name=name,
    )(x, w)


def _forget_kernel(h_ref, wf_ref, b_ref, lf_ref, c_ref, carry_ref, *, tc, lp, pad):
    i = pl.program_id(0)
    off = lax.rem(i * tc, lp)
    fl = lax.dot_general(wf_ref[...], h_ref[...], _NT, preferred_element_type=F32)
    lf = _log_sigmoid(fl + b_ref[...])
    pos = off + lax.broadcasted_iota(jnp.int32, (1, tc), 1)
    valid = pos >= pad
    lfm = jnp.where(valid, lf, 0.0)
    upper = (lax.broadcasted_iota(jnp.int32, (tc, tc), 0)
             <= lax.broadcasted_iota(jnp.int32, (tc, tc), 1)).astype(BF16)
    cs = _dot_f32_lhs(lfm, upper, _NN)

    @pl.when(off == 0)
    def _():
        carry_ref[...] = jnp.zeros_like(carry_ref)

    c = cs + carry_ref[:, 0:1]
    carry_ref[...] = jnp.broadcast_to(c[:, tc - 1:tc], carry_ref.shape)
    lf_ref[...] = lf
    c_ref[...] = jnp.where(valid, c * LOG2E, MASKED_CUMLOG)


def _forget_prompt(h, wf_t, b_col, lp, pad):
    mp, d = h.shape
    nh = wf_t.shape[0]
    tc = _pick_tile(lp, 384, LANES)
    kern = functools.partial(_forget_kernel, tc=tc, lp=lp, pad=pad)
    return pl.pallas_call(
        kern,
        grid=(mp // tc,),
        in_specs=[pl.BlockSpec((tc, d), lambda i: (i, 0)),
                  pl.BlockSpec((nh, d), lambda i: (0, 0)),
                  pl.BlockSpec((nh, 1), lambda i: (0, 0))],
        out_specs=[pl.BlockSpec((nh, tc), lambda i: (0, i)), pl.BlockSpec((nh, tc), lambda i: (0, i))],
        out_shape=[jax.ShapeDtypeStruct((nh, mp), F32), jax.ShapeDtypeStruct((nh, mp), F32)],
        scratch_shapes=[pltpu.VMEM((nh, LANES), F32)],
        compiler_params=_params("arbitrary"),
        name="forget_prompt",
    )(h, wf_t, b_col)


def _attn_kernel(q_ref, k_ref, v_ref, c_ref, who_ref, a_ref, kb_ref, vb_ref, *, t, dh, scale):
    iq = pl.program_id(2)
    hps = who_ref.shape[0]
    ho = who_ref.shape[2]

    @pl.when(iq == 0)
    def _():
        kb_ref[...] = k_ref[...].astype(BF16)
        vb_ref[...] = v_ref[...].astype(BF16)

    qs = [q_ref[:, hh * dh:(hh + 1) * dh].astype(BF16) for hh in range(hps)]
    rows = iq * t + lax.broadcasted_iota(jnp.int32, (t, 1), 0)

    def step(j, carry, diagonal):
        start = pl.multiple_of(j * t, t)
        out = []
        for hh in range(hps):
            m, l, acc = carry[hh]
            ks = kb_ref[pl.ds(start, t), hh * dh:(hh + 1) * dh]
            vs = vb_ref[pl.ds(start, t), hh * dh:(hh + 1) * dh]
            s = lax.dot_general(qs[hh], ks, _NT, preferred_element_type=F32)
            tt = s * (scale * LOG2E) - c_ref[hh, j]
            if diagonal:
                cols = j * t + lax.broadcasted_iota(jnp.int32, (1, t), 1)
                tt = jnp.where(cols <= rows, tt, -jnp.inf)
            m_new = jnp.maximum(m, jnp.max(tt, axis=-1, keepdims=True))
            alpha = jnp.exp2(m - m_new)
            p = jnp.exp2(tt - m_new)
            l = alpha * l + jnp.sum(p, axis=-1, keepdims=True)
            acc = alpha * acc + jnp.dot(p.astype(BF16), vs, preferred_element_type=F32)
            out.append((m_new, l, acc))
        return tuple(out)

    one = (jnp.full((t, 1), -jnp.inf, F32), jnp.zeros((t, 1), F32), jnp.zeros((t, dh), F32))
    carry = lax.fori_loop(0, iq, functools.partial(step, diagonal=False), (one,) * hps)
    carry = step(iq, carry, True)
    for hh in range(hps):
        _, l, acc = carry[hh]
        o = (acc / l).astype(BF16)
        a_ref[:, hh * ho:(hh + 1) * ho] = jnp.dot(o, who_ref[hh], preferred_element_type=F32)


def _attn_prompt(zqkv, c_t, who, nb, lp, nh, dh):
    mp = zqkv.shape[0]
    ho = who.shape[2]
    t = _pick_tile(lp, 384, LANES)
    nq = lp // t
    hps = HEADS_PER_STEP
    ng = nh // hps
    c4 = c_t.reshape(nh, mp // t, 1, t)
    kern = functools.partial(_attn_kernel, t=t, dh=dh, scale=dh ** -0.5)
    return pl.pallas_call(
        kern,
        grid=(nb, ng, nq),
        in_specs=[pl.BlockSpec((t, hps * dh), lambda b, h, i: (b * nq + i, h)),
                  pl.BlockSpec((lp, hps * dh), lambda b, h, i: (b, ng + h)),
                  pl.BlockSpec((lp, hps * dh), lambda b, h, i: (b, 2 * ng + h)),
                  pl.BlockSpec((hps, nq, 1, t), lambda b, h, i: (h, b, 0, 0)),
                  pl.BlockSpec((hps, dh, ho), lambda b, h, i: (h, 0, 0))],
        out_specs=pl.BlockSpec((t, hps * ho), lambda b, h, i: (b * nq + i, h)),
        out_shape=jax.ShapeDtypeStruct((mp, nh * ho), F32),
        scratch_shapes=[pltpu.VMEM((lp, hps * dh), BF16), pltpu.VMEM((lp, hps * dh), BF16)],
        compiler_params=_params("parallel", "parallel", "arbitrary"),
        name="attn_prompt",
    )(zqkv, zqkv, zqkv, c4, who)


def _softmax_update(tiles, m_ref, l_ref, acc_ref, pv):
    mt = tiles[0]
    for t in tiles[1:]:
        mt = jnp.maximum(mt, t)
    m_old = m_ref[:, 0:1]
    m_new = jnp.maximum(m_old, jnp.max(mt, axis=1, keepdims=True))
    alpha = jnp.exp(m_old - m_new)
    ps = [jnp.exp(t - m_new) for t in tiles]
    tot = ps[0]
    for p in ps[1:]:
        tot = tot + p
    l_ref[...] = jnp.broadcast_to(alpha * l_ref[:, 0:1] + jnp.sum(tot, axis=1, keepdims=True), l_ref.shape)
    acc_ref[...] = alpha * acc_ref[...] + pv(ps)
    m_ref[...] = jnp.broadcast_to(m_new, m_ref.shape)


def _sattn_kernel(pt_ref, q_ref, kn_ref, vn_ref, fl_ref, bf_ref, w2_ref, *refs, nh, dh, ps, gp, scale):
    del pt_ref
    k_refs, v_refs, lf_refs = refs[0:gp], refs[gp:2 * gp], refs[2 * gp:3 * gp]
    o_ref, lfo_ref = refs[3 * gp], refs[3 * gp + 1]
    m_ref, l_ref, carry_ref, acc_ref, lfs_ref = refs[3 * gp + 2:]
    step = pl.program_id(1)
    rows = ps * nh
    nchunk = rows // LANES
    lane = lax.broadcasted_iota(jnp.int32, (nh, LANES), 1)
    diag = (lane & (nh - 1)) == lax.broadcasted_iota(jnp.int32, (nh, LANES), 0)

    @pl.when(step == 0)
    def _():
        m_ref[...] = jnp.full_like(m_ref, -jnp.inf)
        l_ref[...] = jnp.zeros_like(l_ref)
        carry_ref[...] = jnp.zeros_like(carry_ref)
        acc_ref[...] = jnp.zeros_like(acc_ref)

    qb = q_ref[0].astype(BF16)

    for g in range(gp):
        lfs_ref[g:g + 1, :] = lf_refs[g][0]
    x = lfs_ref[...]
    w2 = w2_ref[...]
    run = jnp.zeros((gp, LANES), F32)
    cks = []
    for k in range(nchunk):
        r = _dot_f32_lhs(x[:, k * LANES:(k + 1) * LANES], w2, _NN)
        cks.append(r[:, :LANES] + run)
        run = run + r[:, LANES:]
    earlier = (lax.broadcasted_iota(jnp.int32, (gp, gp), 0)
               > lax.broadcasted_iota(jnp.int32, (gp, gp), 1)).astype(BF16)
    offs = _dot_f32_rhs(earlier, run, _NN) + carry_ref[0:1, :]
    carry_new = offs[gp - 1:gp] + run[gp - 1:gp]

    tiles = []
    for g in range(gp):
        kf = k_refs[g][0, 0].reshape(rows, dh).astype(BF16)
        zt = lax.dot_general(qb, kf, _NT, preferred_element_type=F32)
        for k in range(nchunk):
            c_row = cks[k][g:g + 1] + offs[g:g + 1]
            tiles.append(jnp.where(diag, zt[:, k * LANES:(k + 1) * LANES] * scale - c_row, -jnp.inf))

    def pv(probs):
        out = None
        for g in range(gp):
            pm = jnp.concatenate(probs[g * nchunk:(g + 1) * nchunk], axis=1).astype(BF16)
            vf = v_refs[g][0, 0].reshape(rows, dh).astype(BF16)
            o = jnp.dot(pm, vf, preferred_element_type=F32)
            out = o if out is None else out + o
        return out

    _softmax_update(tiles, m_ref, l_ref, acc_ref, pv)
    carry_ref[...] = jnp.broadcast_to(carry_new, carry_ref.shape)

    @pl.when(step == pl.num_programs(1) - 1)
    def _():
        lf_new = _log_sigmoid(fl_ref[0] + bf_ref[...])
        lfo_ref[0] = lf_new
        c_new = carry_new + lf_new
        zeros = jnp.zeros((LANES - nh, dh), F32)
        kpad = jnp.concatenate([kn_ref[0], zeros], axis=0).astype(BF16)
        vpad = jnp.concatenate([vn_ref[0], zeros], axis=0).astype(BF16)
        zt = lax.dot_general(qb, kpad, _NT, preferred_element_type=F32)
        tile = jnp.where(diag & (lane < nh), zt * scale - c_new, -jnp.inf)
        _softmax_update([tile], m_ref, l_ref, acc_ref,
                        lambda probs: jnp.dot(probs[0].astype(BF16), vpad, preferred_element_type=F32))
        o_ref[0] = acc_ref[...] / l_ref[:, 0:1]


def _attn_sample(page_table, zqkv_s, fl_s, bf_pad, cache_k, cache_v, cache_lf, nh, dh):
    db, n_pages = page_table.shape
    n_phys, ps = cache_k.shape[1], cache_k.shape[2]
    assert nh & (nh - 1) == 0 and LANES % nh == 0 and nh % 8 == 0
    rows = ps * nh
    gp = _pick_tile(n_pages, 8, 1)
    qkv_h = zqkv_s.reshape(db, 3 * nh, dh)
    fl3 = fl_s.reshape(db, 1, LANES)
    lf_flat = cache_lf.reshape(n_phys, 1, rows)
    r = jnp.arange(LANES)
    same = (r[:, None] % nh) == (r[None, :] % nh)
    w2 = jnp.concatenate([same & (r[:, None] <= r[None, :]), same], axis=1).astype(BF16)
    kern = functools.partial(_sattn_kernel, nh=nh, dh=dh, ps=ps, gp=gp, scale=dh ** -0.5)

    def page_spec(g):
        return pl.BlockSpec((1, 1, ps, nh, dh), lambda b, s, pt: (0, pt[b, s * gp + g], 0, 0, 0))

    def lf_spec(g):
        return pl.BlockSpec((1, 1, rows), lambda b, s, pt: (pt[b, s * gp + g], 0, 0))

    grid_spec = pltpu.PrefetchScalarGridSpec(
        num_scalar_prefetch=1,
        grid=(db, n_pages // gp),
        in_specs=[pl.BlockSpec((1, nh, dh), lambda b, s, pt: (b, 0, 0)),
                  pl.BlockSpec((1, nh, dh), lambda b, s, pt: (b, 1, 0)),
                  pl.BlockSpec((1, nh, dh), lambda b, s, pt: (b, 2, 0)),
                  pl.BlockSpec((1, 1, LANES), lambda b, s, pt: (b, 0, 0)),
                  pl.BlockSpec((1, LANES), lambda b, s, pt: (0, 0)),
                  pl.BlockSpec((LANES, 2 * LANES), lambda b, s, pt: (0, 0))]
                 + [page_spec(g) for g in range(gp)] + [page_spec(g) for g in range(gp)]
                 + [lf_spec(g) for g in range(gp)],
        out_specs=[pl.BlockSpec((1, nh, dh), lambda b, s, pt: (b, 0, 0)),
                   pl.BlockSpec((1, 1, LANES), lambda b, s, pt: (b, 0, 0))],
        scratch_shapes=[pltpu.VMEM((nh, LANES), F32), pltpu.VMEM((nh, LANES), F32),
                        pltpu.VMEM((8, LANES), F32), pltpu.VMEM((nh, dh), F32),
                        pltpu.VMEM((gp, rows), F32)],
    )
    o3, lf3 = pl.pallas_call(
        kern,
        grid_spec=grid_spec,
        out_shape=[jax.ShapeDtypeStruct((db, nh, dh), F32), jax.ShapeDtypeStruct((db, 1, LANES), F32)],
        compiler_params=_params("parallel", "arbitrary"),
        name="attn_sample",
    )(page_table, qkv_h, qkv_h, qkv_h, fl3, bf_pad, w2,
      *([cache_k] * gp), *([cache_v] * gp), *([lf_flat] * gp))
    return o3.reshape(db, nh * dh), lf3.reshape(db, LANES)[:, :nh]


def _head_out_kernel(o_ref, w_ref, a_ref):
    a_ref[...] = jnp.dot(o_ref[...].astype(BF16), w_ref[0], preferred_element_type=F32)


def _head_out(o, who):
    m = o.shape[0]
    nh, dh, ho = who.shape
    return pl.pallas_call(
        _head_out_kernel,
        grid=(nh,),
        in_specs=[pl.BlockSpec((m, dh), lambda h: (0, h)), pl.BlockSpec((1, dh, ho), lambda h: (h, 0, 0))],
        out_specs=pl.BlockSpec((m, ho), lambda h: (0, h)),
        out_shape=jax.ShapeDtypeStruct((m, nh * ho), F32),
        compiler_params=_params("parallel"),
        name="head_out",
    )(o, who)


def _pool_prompt_kernel(halo_ref, u_ref, d_ref, ext_ref, *, tr, lp, pad, pg):
    i = pl.program_id(0)
    ext_ref[0:HALO, :] = halo_ref[...]
    ext_ref[HALO:, :] = u_ref[...]
    pos = lax.rem(i * tr, lp) - pad + lax.broadcasted_iota(jnp.int32, (tr, 1), 0)
    for g, w in enumerate(POOL_WINDOWS):
        cols = slice(g * pg, (g + 1) * pg)
        u = ext_ref[pl.ds(HALO, tr), cols]
        acc = u
        for k in range(1, w):
            acc = acc + ext_ref[pl.ds(HALO - k, tr), cols]
        cnt = jnp.clip(pos + 1, 1, w).astype(F32)
        d_ref[:, cols] = (acc / cnt - u).astype(d_ref.dtype)


def _pool_prompt(z_rest, pw, lp, pad):
    mp = z_rest.shape[0]
    tr = _pick_tile(lp, 384, BF16_SUBLANES)
    kern = functools.partial(_pool_prompt_kernel, tr=tr, lp=lp, pad=pad, pg=pw // len(POOL_WINDOWS))
    return pl.pallas_call(
        kern,
        grid=(mp // tr,),
        in_specs=[pl.BlockSpec((HALO, pw), lambda i: (jnp.maximum(i * (tr // HALO) - 1, 0), 0)),
                  pl.BlockSpec((tr, pw), lambda i: (i, 0))],
        out_specs=pl.BlockSpec((tr, pw), lambda i: (i, 0)),
        out_shape=jax.ShapeDtypeStruct((mp, pw), BF16),
        scratch_shapes=[pltpu.VMEM((tr + HALO, pw), F32)],
        compiler_params=_params("parallel"),
        name="pool_prompt",
    )(z_rest, z_rest)


def _pool_sample_kernel(st_ref, u_ref, d_ref, *, cnts, pg):
    u_all = u_ref[...]
    n_state = st_ref.shape[0]
    for g, cnt in enumerate(cnts):
        cols = slice(g * pg, (g + 1) * pg)
        u = u_all[:, cols]
        acc = u
        for k in range(1, cnt):
            acc = acc + st_ref[n_state - k, :, cols]
        d_ref[:, cols] = (acc / float(cnt) - u).astype(d_ref.dtype)


def _pool_sample(state_t, z_rest_s, pw, past_len):
    n_state, db, _ = state_t.shape
    tb = _pick_tile(db, 16, BF16_SUBLANES)
    cnts = tuple(min(past_len + 1, w) for w in POOL_WINDOWS)
    kern = functools.partial(_pool_sample_kernel, cnts=cnts, pg=pw // len(POOL_WINDOWS))
    return pl.pallas_call(
        kern,
        grid=(db // tb,),
        in_specs=[pl.BlockSpec((n_state, tb, pw), lambda i: (0, i, 0)),
                  pl.BlockSpec((tb, pw), lambda i: (i, 0))],
        out_specs=pl.BlockSpec((tb, pw), lambda i: (i, 0)),
        out_shape=jax.ShapeDtypeStruct((db, pw), BF16),
        compiler_params=_params("parallel"),
        name="pool_sample",
    )(state_t, z_rest_s)


def _poolmix_kernel(d_ref, w_ref, ga_ref, gp_ref, a_ref, ps_ref, o_ref):
    p = jnp.dot(d_ref[...], w_ref[0], preferred_element_type=F32) * ps_ref[...]
    o_ref[...] = (_sigmoid(ga_ref[...]) * a_ref[...] + _sigmoid(gp_ref[...]) * p).astype(o_ref.dtype)


def _poolmix(d, wpm, z_rest, a, pool_scale, pw):
    m, dm = a.shape
    ng, pg, pgo = wpm.shape
    tm = _pick_tile(m, 1056, BF16_SUBLANES)
    tn = pgo
    assert pw % tn == 0 and dm % tn == 0
    ga0 = pw // tn
    gp0 = (pw + dm) // tn
    return pl.pallas_call(
        _poolmix_kernel,
        grid=(m // tm, ng),
        in_specs=[pl.BlockSpec((tm, pg), lambda i, g: (i, g)),
                  pl.BlockSpec((1, pg, tn), lambda i, g: (g, 0, 0)),
                  pl.BlockSpec((tm, tn), lambda i, g: (i, ga0 + g)),
                  pl.BlockSpec((tm, tn), lambda i, g: (i, gp0 + g)),
                  pl.BlockSpec((tm, tn), lambda i, g: (i, g)),
                  pl.BlockSpec((1, tn), lambda i, g: (0, g))],
        out_specs=pl.BlockSpec((tm, tn), lambda i, g: (i, g)),
        out_shape=jax.ShapeDtypeStruct((m, dm), BF16),
        compiler_params=_params("parallel", "arbitrary"),
        name="poolmix",
    )(d, wpm, z_rest, z_rest, a, pool_scale)


def _conv_gate(a, a1, a2, b, cw_ref, cb_ref):
    cw = cw_ref[...]
    conv = cb_ref[...] + cw[0:1] * a2 + cw[1:2] * a1 + cw[2:3] * a
    return (_gelu_tanh(conv) * b).astype(BF16)


def _ffn_gu_prompt_kernel(h_ref, wg_ref, wu_ref, cw_ref, cb_ref, g_ref, tail_ref, tails_ref, *, tm, tf):
    i = pl.program_id(0)
    j = pl.program_id(1)
    h = h_ref[...]
    a = jnp.dot(h, wg_ref[...], preferred_element_type=F32)
    b = jnp.dot(h, wu_ref[...], preferred_element_type=F32)

    @pl.when(i == 0)
    def _():
        tails_ref[j] = jnp.zeros((CONV_TAIL, tf), F32)

    tail = tails_ref[j]
    row = lax.broadcasted_iota(jnp.int32, (tm, 1), 0)
    a1 = jnp.where(row >= 1, pltpu.roll(a, 1, 0), tail[CONV_TAIL - 1:CONV_TAIL, :])
    a2 = jnp.where(row >= 2, pltpu.roll(a, 2, 0),
                   jnp.where(row == 0, tail[CONV_TAIL - 2:CONV_TAIL - 1, :], tail[CONV_TAIL - 1:CONV_TAIL, :]))
    last = a[tm - CONV_TAIL:, :]
    tails_ref[j] = last
    tail_ref[0] = last
    g_ref[...] = _conv_gate(a, a1, a2, b, cw_ref, cb_ref)


def _ffn_gu_sample_kernel(h_ref, wg_ref, wu_ref, cw_ref, cb_ref, s0_ref, s1_ref, g_ref, a_ref):
    h = h_ref[...]
    a = jnp.dot(h, wg_ref[...], preferred_element_type=F32)
    b = jnp.dot(h, wu_ref[...], preferred_element_type=F32)
    a_ref[...] = a
    g_ref[...] = _conv_gate(a, s1_ref[...], s0_ref[...], b, cw_ref, cb_ref)


def _ffn_gu_specs(tm, dm, tf):
    return [pl.BlockSpec((tm, dm), lambda i, j: (i, 0)),
            pl.BlockSpec((dm, tf), lambda i, j: (0, j)),
            pl.BlockSpec((dm, tf), lambda i, j: (0, j)),
            pl.BlockSpec((3, tf), lambda i, j: (0, j)),
            pl.BlockSpec((1, tf), lambda i, j: (0, j))]


def _ffn_gu_prompt(h2, wg, wu, cw, cb, tf, lp):
    mp, dm = h2.shape
    nj = wg.shape[1] // tf
    tm = _pick_tile(lp, 1408, BF16_SUBLANES)
    kern = functools.partial(_ffn_gu_prompt_kernel, tm=tm, tf=tf)
    return pl.pallas_call(
        kern,
        grid=(mp // tm, nj),
        in_specs=_ffn_gu_specs(tm, dm, tf),
        out_specs=[pl.BlockSpec((tm, tf), lambda i, j: (i, j)),
                   pl.BlockSpec((1, CONV_TAIL, tf), lambda i, j: (i, 0, j))],
        out_shape=[jax.ShapeDtypeStruct((mp, nj * tf), BF16),
                   jax.ShapeDtypeStruct((mp // tm, CONV_TAIL, nj * tf), F32)],
        scratch_shapes=[pltpu.VMEM((nj, CONV_TAIL, tf), F32)],
        compiler_params=_params("arbitrary", "arbitrary"),
        name="ffn_gu_prompt",
    )(h2, wg, wu, cw, cb)


def _ffn_gu_sample(h2, wg, wu, cw, cb, tf, s0, s1):
    m, dm = h2.shape
    nj = wg.shape[1] // tf
    kern = _ffn_gu_sample_kernel
    col = pl.BlockSpec((m, tf), lambda i, j: (0, j))
    return pl.pallas_call(
        kern,
        grid=(1, nj),
        in_specs=_ffn_gu_specs(m, dm, tf) + [col, col],
        out_specs=[col, col],
        out_shape=[jax.ShapeDtypeStruct((m, nj * tf), BF16), jax.ShapeDtypeStruct((m, nj * tf), F32)],
        compiler_params=_params("arbitrary", "arbitrary"),
        name="ffn_gu_sample",
    )(h2, wg, wu, cw, cb, s0, s1)


def kernel(x_prompt, x_sample, cache_k, cache_v, cache_logf, state_pool, state_conv, page_table, meta_tokens, g_pre_mix, g_post_mix, g_pre_ffn, g_post_ffn, w_in, b_forget, w_head_out, w_pool_map, pool_scale, w_out, w_gate, w_up, conv_w, conv_b, w_down):
    nb, seq, dm = x_prompt.shape
    db = x_sample.shape[0]
    depth, _, ps, nh, dh = cache_k.shape
    assert depth == 1 and x_sample.shape[1] == 1
    n_meta = meta_tokens.shape[0]
    n_state, pw = state_pool.shape[2], state_pool.shape[3]
    dff = w_gate.shape[2]
    aw = nh * dh
    past_len = page_table.shape[1] * ps
    assert n_state <= HALO - 1 and w_in.shape[2] == 3 * aw + nh + pw + 2 * dm

    pad = (-(n_meta + seq)) % LANES
    if pad < HALO:
        pad += LANES
    lp = pad + n_meta + seq
    mp = nb * lp

    w_in_bf = w_in[0].astype(BF16)
    w_f = w_in[0, :, 3 * aw:3 * aw + nh]
    w_rest = w_in_bf[:, 3 * aw + nh:]
    wf_t = w_f.T.astype(BF16)
    wf_pad = jnp.pad(w_f, ((0, 0), (0, LANES - nh))).astype(BF16)
    who = w_head_out[0].astype(BF16)
    wpm = w_pool_map[0].astype(BF16)
    wo = w_out[0].astype(BF16)
    tf = _pick_tile(dff, 256, LANES)
    wg = w_gate[0].astype(BF16)
    wu = w_up[0].astype(BF16)
    wd = w_down[0].astype(BF16)
    cw = conv_w[0]
    cb = conv_b[0].reshape(1, dff)
    g1, g2, g3, g4 = (g[0].reshape(1, dm) for g in (g_pre_mix, g_post_mix, g_pre_ffn, g_post_ffn))
    pscale = pool_scale[0].reshape(1, dm)
    bf_col = b_forget[0].reshape(nh, 1)
    bf_pad = jnp.pad(b_forget[0].reshape(1, nh), ((0, 0), (0, LANES - nh)))

    head = jnp.concatenate([jnp.zeros((pad, dm), F32), meta_tokens], axis=0)
    assert seq % head.shape[0] == 0
    h1 = _rmsnorm_prompt(head, x_prompt, g1, lp)
    zqkv = _matmul(h1, w_in_bf, n_cols=3 * aw)
    z_rest = _matmul(h1, w_rest)
    lf_t, c_t = _forget_prompt(h1, wf_t, bf_col, lp, pad)
    a = _attn_prompt(zqkv, c_t, who, nb, lp, nh, dh)
    d = _pool_prompt(z_rest, pw, lp, pad)
    mix = _poolmix(d, wpm, z_rest, a, pscale, pw)
    mm = _matmul(mix, wo)
    x2, h2 = _post_mix_prompt(mm, head, x_prompt, g2, g3, lp, pad)
    gact, tails = _ffn_gu_prompt(h2, wg, wu, cw, cb, tf, lp)
    f = _matmul(gact, wd, tm_cap=704, x_buffers=1, name="ffn_down")
    y_prompt = _post_ffn_prompt(f, x2, g4, nb, seq, lp, head.shape[0])

    k_prompt = zqkv[:, aw:2 * aw].reshape(nb, lp, nh, dh)[:, pad:][None]
    v_prompt = zqkv[:, 2 * aw:].reshape(nb, lp, nh, dh)[:, pad:][None]
    logf_prompt = lf_t.T.reshape(nb, lp, nh)[:, pad:][None]
    pool_prompt = z_rest.reshape(nb, lp, -1)[:, lp - n_state:, :pw][None]
    tiles_per_seq = tails.shape[0] // nb
    conv_prompt = tails.reshape(nb, tiles_per_seq, CONV_TAIL, dff)[:, -1, CONV_TAIL - 2:][None]

    xs = x_sample.reshape(db, dm)
    h1s = _rmsnorm(xs, g1, BF16)
    zqkv_s = _matmul(h1s, w_in_bf, n_cols=3 * aw)
    z_rest_s = _matmul(h1s, w_rest)
    fl_s = _matmul(h1s, wf_pad)
    o_s, lf_s = _attn_sample(page_table, zqkv_s, fl_s, bf_pad, cache_k, cache_v, cache_logf[0], nh, dh)
    a_s = _head_out(o_s, who)
    d_s = _pool_sample(state_pool[0].transpose(1, 0, 2), z_rest_s, pw, past_len)
    mix_s = _poolmix(d_s, wpm, z_rest_s, a_s, pscale, pw)
    mm_s = _matmul(mix_s, wo)
    x2s, h2s = _post_mix(mm_s, xs, g2, g3)
    gact_s, a_new = _ffn_gu_sample(h2s, wg, wu, cw, cb, tf, state_conv[0][:, 0], state_conv[0][:, 1])
    f_s = _matmul(gact_s, wd, name="ffn_down")
    y_s = _post_ffn(f_s, x2s, g4)

    y_sample = y_s.reshape(db, 1, dm)
    k_sample = zqkv_s[:, aw:2 * aw].reshape(1, db, 1, nh, dh)
    v_sample = zqkv_s[:, 2 * aw:].reshape(1, db, 1, nh, dh)
    logf_sample = lf_s.reshape(1, db, 1, nh)
    u_s = z_rest_s[:, :pw]
    pool_sample = jnp.concatenate([state_pool[0][:, 1:], u_s[:, None]], axis=1)[None]
    conv_sample = jnp.stack([state_conv[0][:, 1], a_new], axis=1)[None]

    return (y_prompt, y_sample, k_prompt, v_prompt, logf_prompt, pool_prompt, conv_prompt,
            k_sample, v_sample, logf_sample, pool_sample, conv_sample)
```
